```python
import jax, jax.numpy as jnp
from jax import lax
import numpy as np

D_MODEL = 1024
BATCH = 8
SEQ = 4096
DEPTH = 4

CHUNK = 64
Q_BLOCK = 2 * CHUNK
N_MIXERS = 2
CONV_WIDTH = 3
MIX_WIDTH = D_MODEL
SB_HEADS = 16
SB_HEAD_DIM = MIX_WIDTH // SB_HEADS
N_MEM = 256
MEM_HEADS = 4
MEM_HEAD_DIM = 128
MEM_WIDTH = MEM_HEADS * MEM_HEAD_DIM
IN_WIDTH = 3 * MIX_WIDTH + MEM_WIDTH
OUT_WIDTH = MIX_WIDTH + MEM_WIDTH
D_FF = -(-(8 * D_MODEL) // (3 * 256)) * 256
N_CONV_LAYERS = (DEPTH + N_MIXERS - 1) // N_MIXERS
EPS = 1e-6

kernel_name = "hybrid_shortconv_stickbreaking_memxattn"


def rmsnorm(x, g):
    xf = x.astype(jnp.float32)
    y = xf * lax.rsqrt(jnp.mean(xf * xf, axis=-1, keepdims=True) + EPS)
    return (y * g.astype(jnp.float32)).astype(x.dtype)


def short_conv_mixer(b_gate, c_gate, u, w_conv):
    y = c_gate * u
    y = lax.conv_general_dilated(
        y, w_conv[:, None, :].astype(y.dtype),
        window_strides=(1,),
        padding=[(CONV_WIDTH - 1, 0)],
        dimension_numbers=("NWC", "WIO", "NWC"),
        feature_group_count=y.shape[-1])
    return b_gate * y


def stick_breaking_attention(q, k, v):
    b, h, s, dh = q.shape
    n_blk = s // Q_BLOCK
    scale = dh ** -0.5
    kf = k.astype(jnp.float32)
    vf = v.astype(jnp.float32)
    q_blocks = q.astype(jnp.float32).reshape(b, h, n_blk, Q_BLOCK, dh).transpose(2, 0, 1, 3, 4)
    key_pos = jnp.arange(s)

    def block(args):
        qb, t0 = args
        t = t0 + jnp.arange(Q_BLOCK)
        mask = key_pos[None, :] < t[:, None]
        z = jnp.einsum("bhqd,bhkd->bhqk", qb, kf) * scale
        log_keep = jnp.where(mask, jax.nn.log_sigmoid(-z), 0.0)
        later = lax.cumsum(log_keep, axis=3, reverse=True) - log_keep
        log_a = jax.nn.log_sigmoid(z) + later
        a = jnp.where(mask, jnp.exp(log_a), 0.0)
        return jnp.einsum("bhqk,bhkd->bhqd", a, vf)

    out = lax.map(block, (q_blocks, jnp.arange(n_blk) * Q_BLOCK))
    return out.transpose(1, 2, 0, 3, 4).reshape(b, h, s, dh).astype(q.dtype)


def memory_cross_attention(q, mem_kv, g_q, g_k):
    b, s, _ = q.shape
    m = mem_kv.shape[1]
    k, v = jnp.split(mem_kv, 2, axis=-1)
    qh = rmsnorm(q.reshape(b, s, MEM_HEADS, MEM_HEAD_DIM), g_q)
    kh = rmsnorm(k.reshape(b, m, MEM_HEADS, MEM_HEAD_DIM), g_k)
    vh = v.reshape(b, m, MEM_HEADS, MEM_HEAD_DIM)
    scores = jnp.einsum("bshd,bmhd->bhsm", qh, kh).astype(jnp.float32) * (MEM_HEAD_DIM ** -0.5)
    p = jax.nn.softmax(scores, axis=-1).astype(vh.dtype)
    o = jnp.einsum("bhsm,bmhd->bshd", p, vh)
    return o.reshape(b, s, MEM_WIDTH)


def setup_inputs(seed: int = 0) -> dict:
    key = jax.random.key(seed)
    ks = jax.random.split(key, 13)
    f32 = jnp.float32
    x = jax.random.normal(ks[0], (BATCH, SEQ, D_MODEL), f32)
    mem = jax.random.normal(ks[1], (BATCH, N_MEM, D_MODEL), f32)
    w_in = jax.random.normal(ks[2], (DEPTH, D_MODEL, IN_WIDTH), f32) * D_MODEL ** -0.5
    w_out = jax.random.normal(ks[3], (DEPTH, OUT_WIDTH, D_MODEL), f32) * OUT_WIDTH ** -0.5
    w_mem_kv = jax.random.normal(ks[4], (DEPTH, D_MODEL, 2 * MEM_WIDTH), f32) * D_MODEL ** -0.5
    w_conv = jax.random.normal(ks[5], (N_CONV_LAYERS, CONV_WIDTH, MIX_WIDTH), f32) * CONV_WIDTH ** -0.5
    w_ffn_in = jax.random.normal(ks[6], (DEPTH, D_MODEL, 2 * D_FF), f32) * D_MODEL ** -0.5
    w_ffn_out = jax.random.normal(ks[7], (DEPTH, D_FF, D_MODEL), f32) * D_FF ** -0.5
    g_mix = 1.0 + 0.02 * jax.random.normal(ks[8], (DEPTH, D_MODEL), f32)
    g_mem = 1.0 + 0.02 * jax.random.normal(ks[9], (DEPTH, D_MODEL), f32)
    g_ffn = 1.0 + 0.02 * jax.random.normal(ks[10], (DEPTH, D_MODEL), f32)
    g_q = 1.0 + 0.02 * jax.random.normal(ks[11], (DEPTH, MEM_HEAD_DIM), f32)
    g_k = 1.0 + 0.02 * jax.random.normal(ks[12], (DEPTH, MEM_HEAD_DIM), f32)
    return {"x": x, "mem": mem, "w_in": w_in, "w_out": w_out, "w_mem_kv": w_mem_kv,
            "w_conv": w_conv, "w_ffn_in": w_ffn_in, "w_ffn_out": w_ffn_out,
            "g_mix": g_mix, "g_mem": g_mem, "g_ffn": g_ffn, "g_q": g_q, "g_k": g_k}


def reference(x, mem, w_in, w_out, w_mem_kv, w_conv, w_ffn_in, w_ffn_out,
              g_mix, g_mem, g_ffn, g_q, g_k):
    b, s, _ = x.shape
    for i in range(DEPTH):
        h = rmsnorm(x, g_mix[i])
        proj = h @ w_in[i]
        p0, p1, p2 = jnp.split(proj[..., :3 * MIX_WIDTH], 3, axis=-1)
        q_mem = proj[..., 3 * MIX_WIDTH:]
        if i % N_MIXERS == 0:
            mixed = short_conv_mixer(p0, p1, p2, w_conv[i // N_MIXERS])
        else:
            def heads(t):
                return t.reshape(b, s, SB_HEADS, SB_HEAD_DIM).transpose(0, 2, 1, 3)
            o = stick_breaking_attention(heads(p0), heads(p1), heads(p2))
            mixed = o.transpose(0, 2, 1, 3).reshape(b, s, MIX_WIDTH)
        mem_kv = rmsnorm(mem, g_mem[i]) @ w_mem_kv[i]
        mem_out = memory_cross_attention(q_mem, mem_kv, g_q[i], g_k[i])
        x = x + jnp.concatenate([mixed, mem_out], axis=-1) @ w_out[i]
        h = rmsnorm(x, g_ffn[i])
        gate, up = jnp.split(h @ w_ffn_in[i], 2, axis=-1)
        x = x + (jax.nn.silu(gate) * up) @ w_ffn_out[i]
    return x
```

```python
import functools

import jax
import jax.numpy as jnp
from jax import lax
from jax.experimental import pallas as pl
from jax.experimental.pallas import tpu as pltpu

D_MODEL = 1024
DEPTH = 4
N_MIXERS = 2
CONV_WIDTH = 3
MIX_WIDTH = D_MODEL
SB_HEADS = 16
SB_HEAD_DIM = MIX_WIDTH // SB_HEADS
N_MEM = 256
MEM_HEADS = 4
MEM_HEAD_DIM = 128
MEM_WIDTH = MEM_HEADS * MEM_HEAD_DIM
IN_WIDTH = 3 * MIX_WIDTH + MEM_WIDTH
OUT_WIDTH = MIX_WIDTH + MEM_WIDTH
D_FF = -(-(8 * D_MODEL) // (3 * 256)) * 256
EPS = 1e-6

LANES = 128
SUBLANES = 8
VMEM_LIMIT = 56 * 1024 * 1024

TM = 512
TQ = 256
TK = 256
FFN_CHUNKS = 2
HEADS_PER_STEP = LANES // SB_HEAD_DIM

BF16 = jnp.bfloat16
F32 = jnp.float32


def _dot(a, b):
    return jnp.dot(a, b, preferred_element_type=F32)


def _dot_nt(a, b):
    return lax.dot_general(a, b, (((1,), (1,)), ((), ())), preferred_element_type=F32)


def _rms(x):
    return x * lax.rsqrt(jnp.mean(x * x, axis=-1, keepdims=True) + EPS)


def _resident(block_shape, index_map):
    return pl.BlockSpec(block_shape, index_map, pipeline_mode=pl.Buffered(1))


def _mem_kv_kernel(mem_ref, g_ref, w_ref, gk_ref, knt_ref, v_ref):
    h = (_rms(mem_ref[0]) * g_ref[0]).astype(BF16)
    kv = _dot(h, w_ref[0])
    gk = gk_ref[0]
    for hd in range(MEM_HEADS):
        lo = hd * MEM_HEAD_DIM
        kn = _rms(kv[:, lo:lo + MEM_HEAD_DIM]) * gk
        knt_ref[0, 0, lo:lo + MEM_HEAD_DIM, :] = kn.T.astype(BF16)
    v_ref[0, 0] = kv[:, MEM_WIDTH:].astype(BF16)


def _mem_kv(mem, g_mem, w_mem_kv, g_k):
    b = mem.shape[0]
    return pl.pallas_call(
        _mem_kv_kernel,
        grid=(DEPTH, b),
        in_specs=[
            pl.BlockSpec((1, N_MEM, D_MODEL), lambda d, i: (i, 0, 0)),
            pl.BlockSpec((1, 1, D_MODEL), lambda d, i: (d, 0, 0)),
            pl.BlockSpec((1, D_MODEL, 2 * MEM_WIDTH), lambda d, i: (d, 0, 0)),
            pl.BlockSpec((1, 1, MEM_HEAD_DIM), lambda d, i: (d, 0, 0)),
        ],
        out_specs=[
            pl.BlockSpec((1, 1, MEM_WIDTH, N_MEM), lambda d, i: (d, i, 0, 0)),
            pl.BlockSpec((1, 1, N_MEM, MEM_WIDTH), lambda d, i: (d, i, 0, 0)),
        ],
        out_shape=[
            jax.ShapeDtypeStruct((DEPTH, b, MEM_WIDTH, N_MEM), BF16),
            jax.ShapeDtypeStruct((DEPTH, b, N_MEM, MEM_WIDTH), BF16),
        ],
        compiler_params=pltpu.CompilerParams(
            dimension_semantics=("arbitrary", "arbitrary"), vmem_limit_bytes=VMEM_LIMIT),
        name="mem_kv",
    )(mem, g_mem.reshape(DEPTH, 1, D_MODEL), w_mem_kv, g_k.reshape(DEPTH, 1, MEM_HEAD_DIM))


def _mem_attention(h, w_ref, gq_ref, knt_ref, v_ref, out_ref):
    qm = _dot(h, w_ref[:, 3 * MIX_WIDTH:])
    gq = gq_ref[...]
    for hd in range(MEM_HEADS):
        lo = hd * MEM_HEAD_DIM
        qn = (_rms(qm[:, lo:lo + MEM_HEAD_DIM]) * gq).astype(BF16)
        s = _dot(qn, knt_ref[0, lo:lo + MEM_HEAD_DIM, :]) * (MEM_HEAD_DIM ** -0.5)
        e = jnp.exp(s - jnp.max(s, axis=-1, keepdims=True))
        inv = 1.0 / jnp.sum(e, axis=-1, keepdims=True)
        o = _dot(e.astype(BF16), v_ref[0, :, lo:lo + MEM_HEAD_DIM]) * inv
        out_ref[0, :, lo:lo + MEM_HEAD_DIM] = o.astype(BF16)


def _in_conv_kernel(x_ref, g_ref, w_ref, wc_ref, gq_ref, knt_ref, v_ref, mix_ref, mo_ref, ybuf):
    @pl.when(pl.program_id(1) == 0)
    def _():
        ybuf[0:SUBLANES, :] = jnp.zeros((SUBLANES, MIX_WIDTH), F32)

    h = (_rms(x_ref[0]) * g_ref[...]).astype(BF16)
    y = _dot(h, w_ref[:, MIX_WIDTH:2 * MIX_WIDTH]) * _dot(h, w_ref[:, 2 * MIX_WIDTH:3 * MIX_WIDTH])
    ybuf[SUBLANES:SUBLANES + TM, :] = y
    conv = (wc_ref[2:3, :] * y
            + wc_ref[1:2, :] * ybuf[SUBLANES - 1:SUBLANES - 1 + TM, :]
            + wc_ref[0:1, :] * ybuf[SUBLANES - 2:SUBLANES - 2 + TM, :])
    ybuf[0:SUBLANES, :] = ybuf[TM:TM + SUBLANES, :]
    mix_ref[0] = (_dot(h, w_ref[:, 0:MIX_WIDTH]) * conv).astype(BF16)
    _mem_attention(h, w_ref, gq_ref, knt_ref, v_ref, mo_ref)


def _in_sb_kernel(x_ref, g_ref, w_ref, gq_ref, knt_ref, v_ref, qkv_ref, mo_ref):
    h = (_rms(x_ref[0]) * g_ref[...]).astype(BF16)
    qkv_ref[0, :, 0:MIX_WIDTH] = (_dot(h, w_ref[:, 0:MIX_WIDTH]) * (SB_HEAD_DIM ** -0.5)).astype(BF16)
    qkv_ref[0, :, MIX_WIDTH:3 * MIX_WIDTH] = _dot(h, w_ref[:, MIX_WIDTH:3 * MIX_WIDTH]).astype(BF16)
    _mem_attention(h, w_ref, gq_ref, knt_ref, v_ref, mo_ref)


def _in_proj(x, g_mix, w_in, w_conv, g_q, knt, v, conv_layer):
    b, s, _ = x.shape
    tok = lambda i, j: (i, j, 0)
    const2 = lambda i, j: (0, 0)
    per_batch = lambda i, j: (i, 0, 0)
    in_specs = [
        pl.BlockSpec((1, TM, D_MODEL), tok),
        _resident((1, D_MODEL), const2),
        _resident((D_MODEL, IN_WIDTH), const2),
    ]
    args = [x, g_mix.reshape(1, D_MODEL), w_in]
    if conv_layer:
        in_specs.append(_resident((CONV_WIDTH, MIX_WIDTH), const2))
        args.append(w_conv)
    in_specs += [
        _resident((1, MEM_HEAD_DIM), const2),
        pl.BlockSpec((1, MEM_WIDTH, N_MEM), per_batch),
        pl.BlockSpec((1, N_MEM, MEM_WIDTH), per_batch),
    ]
    args += [g_q.reshape(1, MEM_HEAD_DIM), knt, v]
    mix_width = MIX_WIDTH if conv_layer else 3 * MIX_WIDTH
    return pl.pallas_call(
        _in_conv_kernel if conv_layer else _in_sb_kernel,
        grid=(b, s // TM),
        in_specs=in_specs,
        out_specs=[
            pl.BlockSpec((1, TM, mix_width), tok),
            pl.BlockSpec((1, TM, MEM_WIDTH), tok),
        ],
        out_shape=[
            jax.ShapeDtypeStruct((b, s, mix_width), BF16),
            jax.ShapeDtypeStruct((b, s, MEM_WIDTH), BF16),
        ],
        scratch_shapes=[pltpu.VMEM((SUBLANES + TM, MIX_WIDTH), F32)] if conv_layer else [],
        compiler_params=pltpu.CompilerParams(
            dimension_semantics=("arbitrary", "arbitrary"), vmem_limit_bytes=VMEM_LIMIT),
        name="in_conv" if conv_layer else "in_sb",
    )(*args)


def _sb_kernel(q_ref, k_ref, v_ref, o_ref, acc_ref, c_ref):
    qi = pl.program_id(2)
    lane = lax.broadcasted_iota(jnp.int32, (1, LANES), 1)
    head_masks = [(lane >= hh * SB_HEAD_DIM) & (lane < (hh + 1) * SB_HEAD_DIM)
                  for hh in range(HEADS_PER_STEP)]
    q2 = q_ref[0]
    q_heads = [jnp.where(m, q2, jnp.zeros_like(q2)) for m in head_masks]
    tri = (lax.broadcasted_iota(jnp.int32, (TK, TK), 0)
           >= lax.broadcasted_iota(jnp.int32, (TK, TK), 1)).astype(BF16)
    tri2 = jnp.concatenate([tri, tri], axis=0)

    acc_ref[...] = jnp.zeros_like(acc_ref)
    c_ref[...] = jnp.zeros_like(c_ref)

    def sweep_block(start, mask):
        k2 = k_ref[0, pl.ds(start, TK), :]
        v2 = v_ref[0, pl.ds(start, TK), :]
        weights = []
        for hh in range(HEADS_PER_STEP):
            z = _dot_nt(q_heads[hh], k2)
            sp = jnp.maximum(z, 0.0) + jnp.log(1.0 + jnp.exp(jnp.minimum(z, -z)))
            if mask is not None:
                sp = jnp.where(mask, sp, 0.0)
            sp_hi = sp.astype(BF16)
            sp_lo = (sp - sp_hi.astype(F32)).astype(BF16)
            suffix = _dot(jnp.concatenate([sp_hi, sp_lo], axis=1), tri2)
            a = jnp.exp(z - suffix - c_ref[hh])
            if mask is not None:
                a = jnp.where(mask, a, 0.0)
            c_ref[hh] = c_ref[hh] + suffix[:, 0:1]
            weights.append(a.astype(BF16))
        v_heads = jnp.concatenate([jnp.where(m, v2, jnp.zeros_like(v2)) for m in head_masks], axis=0)
        acc_ref[...] += _dot(jnp.concatenate(weights, axis=1), v_heads)

    n_diag = TQ // TK
    row = lax.broadcasted_iota(jnp.int32, (TQ, TK), 0)
    col = lax.broadcasted_iota(jnp.int32, (TQ, TK), 1)
    for d in range(n_diag):
        off = (n_diag - 1 - d) * TK
        sweep_block(pl.multiple_of(qi * TQ + off, TK), col + off < row)

    n_prev = qi * n_diag

    def body(i, carry):
        sweep_block(pl.multiple_of((n_prev - 1 - i) * TK, TK), None)
        return carry

    lax.fori_loop(0, n_prev, body, 0)
    o_ref[0] = acc_ref[...].astype(BF16)


def _stick_breaking(qkv):
    b, s, _ = qkv.shape
    n_col = MIX_WIDTH // LANES
    return pl.pallas_call(
        _sb_kernel,
        grid=(b, n_col, s // TQ),
        in_specs=[
            pl.BlockSpec((1, TQ, LANES), lambda i, p, j: (i, j, p)),
            pl.BlockSpec((1, s, LANES), lambda i, p, j: (i, 0, n_col + p)),
            pl.BlockSpec((1, s, LANES), lambda i, p, j: (i, 0, 2 * n_col + p)),
        ],
        out_specs=pl.BlockSpec((1, TQ, LANES), lambda i, p, j: (i, j, p)),
        out_shape=jax.ShapeDtypeStruct((b, s, MIX_WIDTH), BF16),
        scratch_shapes=[
            pltpu.VMEM((TQ, LANES), F32),
            pltpu.VMEM((HEADS_PER_STEP, TQ, 1), F32),
        ],
        compiler_params=pltpu.CompilerParams(
            dimension_semantics=("arbitrary", "arbitrary", "arbitrary"),
            vmem_limit_bytes=VMEM_LIMIT),
        name="stick_breaking",
    )(qkv, qkv, qkv)


def _out_ffn_kernel(x_ref, mix_ref, mo_ref, wo_ref, g_ref, wi_ref, wf_ref, out_ref):
    x1 = (x_ref[0] + _dot(mix_ref[0], wo_ref[0:MIX_WIDTH, :])
          + _dot(mo_ref[0], wo_ref[MIX_WIDTH:OUT_WIDTH, :]))
    h = (_rms(x1) * g_ref[...]).astype(BF16)
    ffn = None
    chunk = D_FF // FFN_CHUNKS
    for c in range(FFN_CHUNKS):
        lo = c * chunk
        gate = _dot(h, wi_ref[:, lo:lo + chunk])
        up = _dot(h, wi_ref[:, D_FF + lo:D_FF + lo + chunk])
        act = (gate * jax.nn.sigmoid(gate) * up).astype(BF16)
        part = _dot(act, wf_ref[lo:lo + chunk, :])
        ffn = part if ffn is None else ffn + part
    out_ref[0] = ffn + x1


def _out_ffn(x, mixed, mem_out, w_out, g_ffn, w_ffn_in, w_ffn_out):
    b, s, _ = x.shape
    tok = lambda i, j: (i, j, 0)
    const2 = lambda i, j: (0, 0)
    return pl.pallas_call(
        _out_ffn_kernel,
        grid=(b, s // TM),
        in_specs=[
            pl.BlockSpec((1, TM, D_MODEL), tok),
            pl.BlockSpec((1, TM, MIX_WIDTH), tok),
            pl.BlockSpec((1, TM, MEM_WIDTH), tok),
            _resident((OUT_WIDTH, D_MODEL), const2),
            _resident((1, D_MODEL), const2),
            _resident((D_MODEL, 2 * D_FF), const2),
            _resident((D_FF, D_MODEL), const2),
        ],
        out_specs=pl.BlockSpec((1, TM, D_MODEL), tok),
        out_shape=jax.ShapeDtypeStruct((b, s, D_MODEL), F32),
        compiler_params=pltpu.CompilerParams(
            dimension_semantics=("arbitrary", "arbitrary"), vmem_limit_bytes=VMEM_LIMIT),
        name="out_ffn",
    )(x, mixed, mem_out, w_out, g_ffn.reshape(1, D_MODEL), w_ffn_in, w_ffn_out)


def kernel(x, mem, w_in, w_out, w_mem_kv, w_conv, w_ffn_in, w_ffn_out, g_mix, g_mem, g_ffn, g_q, g_k):
    assert x.shape[1] % TM == 0 and x.shape[1] % TQ == 0 and TQ % TK == 0
    assert D_FF % (FFN_CHUNKS * LANES) == 0
    w_in, w_out, w_mem_kv, w_ffn_in, w_ffn_out = (
        w.astype(BF16) for w in (w_in, w_out, w_mem_kv, w_ffn_in, w_ffn_out))
    knt, v = _mem_kv(mem, g_mem, w_mem_kv, g_k)
    for i in range(DEPTH):
        conv_layer = i % N_MIXERS == 0
        mixed, mem_out = _in_proj(x, g_mix[i], w_in[i], w_conv[i // N_MIXERS] if conv_layer else None,
                                  g_q[i], knt[i], v[i], conv_layer)
        if not conv_layer:
            mixed = _stick_breaking(mixed)
        x = _out_ffn(x, mixed, mem_out, w_out[i], g_ffn[i], w_ffn_in[i], w_ffn_out[i])
    return x
```

```python
import math

import jax
import jax.numpy as jnp
from jax import lax
from jax.experimental import pallas as pl
from jax.experimental.pallas import tpu as pltpu

D_MODEL = 1024
DEPTH = 4
N_MIXERS = 2
CONV_WIDTH = 3
MIX_WIDTH = D_MODEL
SB_HEADS = 16
SB_HEAD_DIM = MIX_WIDTH // SB_HEADS
N_MEM = 256
MEM_HEADS = 4
MEM_HEAD_DIM = 128
MEM_WIDTH = MEM_HEADS * MEM_HEAD_DIM
IN_WIDTH = 3 * MIX_WIDTH + MEM_WIDTH
OUT_WIDTH = MIX_WIDTH + MEM_WIDTH
D_FF = -(-(8 * D_MODEL) // (3 * 256)) * 256
EPS = 1e-6

LANES = 128
SUBLANES = 8
VMEM_LIMIT = 56 * 1024 * 1024

TM = 512
TQ = 256
TK = 256
FFN_CHUNKS = 2
HEADS_PER_COL = LANES // SB_HEAD_DIM
SB_COLS = 2
LOG2E = math.log2(math.e)
SKIP_LOG2 = 160.0

BF16 = jnp.bfloat16
F32 = jnp.float32


def _dot(a, b):
    return jnp.dot(a, b, preferred_element_type=F32)


def _dot_nt(a, b):
    return lax.dot_general(a, b, (((1,), (1,)), ((), ())), preferred_element_type=F32)


def _rms(x):
    return x * lax.rsqrt(jnp.mean(x * x, axis=-1, keepdims=True) + EPS)


def _resident(block_shape, index_map):
    return pl.BlockSpec(block_shape, index_map, pipeline_mode=pl.Buffered(1))


def _mem_kv_kernel(mem_ref, g_ref, w_ref, gk_ref, knt_ref, v_ref):
    h = (_rms(mem_ref[0]) * g_ref[0]).astype(BF16)
    kv = _dot(h, w_ref[0])
    gk = gk_ref[0]
    for hd in range(MEM_HEADS):
        lo = hd * MEM_HEAD_DIM
        kn = _rms(kv[:, lo:lo + MEM_HEAD_DIM]) * gk
        knt_ref[0, 0, lo:lo + MEM_HEAD_DIM, :] = kn.T.astype(BF16)
    v_ref[0, 0] = kv[:, MEM_WIDTH:].astype(BF16)


def _mem_kv(mem, g_mem, w_mem_kv, g_k):
    b = mem.shape[0]
    return pl.pallas_call(
        _mem_kv_kernel,
        grid=(DEPTH, b),
        in_specs=[
            pl.BlockSpec((1, N_MEM, D_MODEL), lambda d, i: (i, 0, 0)),
            pl.BlockSpec((1, 1, D_MODEL), lambda d, i: (d, 0, 0)),
            pl.BlockSpec((1, D_MODEL, 2 * MEM_WIDTH), lambda d, i: (d, 0, 0)),
            pl.BlockSpec((1, 1, MEM_HEAD_DIM), lambda d, i: (d, 0, 0)),
        ],
        out_specs=[
            pl.BlockSpec((1, 1, MEM_WIDTH, N_MEM), lambda d, i: (d, i, 0, 0)),
            pl.BlockSpec((1, 1, N_MEM, MEM_WIDTH), lambda d, i: (d, i, 0, 0)),
        ],
        out_shape=[
            jax.ShapeDtypeStruct((DEPTH, b, MEM_WIDTH, N_MEM), BF16),
            jax.ShapeDtypeStruct((DEPTH, b, N_MEM, MEM_WIDTH), BF16),
        ],
        compiler_params=pltpu.CompilerParams(
            dimension_semantics=("arbitrary", "arbitrary"), vmem_limit_bytes=VMEM_LIMIT),
        name="mem_kv",
    )(mem, g_mem.reshape(DEPTH, 1, D_MODEL), w_mem_kv, g_k.reshape(DEPTH, 1, MEM_HEAD_DIM))


def _mem_attention(h, w_ref, gq_ref, knt_ref, v_ref, out_ref):
    qm = _dot(h, w_ref[:, 3 * MIX_WIDTH:])
    gq = gq_ref[...]
    for hd in range(MEM_HEADS):
        lo = hd * MEM_HEAD_DIM
        qn = (_rms(qm[:, lo:lo + MEM_HEAD_DIM]) * gq).astype(BF16)
        s = _dot(qn, knt_ref[0, lo:lo + MEM_HEAD_DIM, :]) * (MEM_HEAD_DIM ** -0.5)
        e = jnp.exp(s - jnp.max(s, axis=-1, keepdims=True))
        inv = 1.0 / jnp.sum(e, axis=-1, keepdims=True)
        o = _dot(e.astype(BF16), v_ref[0, :, lo:lo + MEM_HEAD_DIM]) * inv
        out_ref[0, :, lo:lo + MEM_HEAD_DIM] = o.astype(BF16)


def _in_conv_kernel(x_ref, g_ref, w_ref, wc_ref, gq_ref, knt_ref, v_ref, mix_ref, mo_ref, ybuf):
    @pl.when(pl.program_id(1) == 0)
    def _():
        ybuf[0:SUBLANES, :] = jnp.zeros((SUBLANES, MIX_WIDTH), F32)

    h = (_rms(x_ref[0]) * g_ref[...]).astype(BF16)
    y = _dot(h, w_ref[:, MIX_WIDTH:2 * MIX_WIDTH]) * _dot(h, w_ref[:, 2 * MIX_WIDTH:3 * MIX_WIDTH])
    ybuf[SUBLANES:SUBLANES + TM, :] = y
    conv = (wc_ref[2:3, :] * y
            + wc_ref[1:2, :] * ybuf[SUBLANES - 1:SUBLANES - 1 + TM, :]
            + wc_ref[0:1, :] * ybuf[SUBLANES - 2:SUBLANES - 2 + TM, :])
    ybuf[0:SUBLANES, :] = ybuf[TM:TM + SUBLANES, :]
    mix_ref[0] = (_dot(h, w_ref[:, 0:MIX_WIDTH]) * conv).astype(BF16)
    _mem_attention(h, w_ref, gq_ref, knt_ref, v_ref, mo_ref)


def _in_sb_kernel(x_ref, g_ref, w_ref, gq_ref, knt_ref, v_ref, qkv_ref, mo_ref):
    h = (_rms(x_ref[0]) * g_ref[...]).astype(BF16)
    qkv_ref[0, :, 0:MIX_WIDTH] = (
        _dot(h, w_ref[:, 0:MIX_WIDTH]) * (LOG2E * SB_HEAD_DIM ** -0.5)).astype(BF16)
    qkv_ref[0, :, MIX_WIDTH:3 * MIX_WIDTH] = _dot(h, w_ref[:, MIX_WIDTH:3 * MIX_WIDTH]).astype(BF16)
    _mem_attention(h, w_ref, gq_ref, knt_ref, v_ref, mo_ref)


def _in_proj(x, g_mix, w_in, w_conv, g_q, knt, v, conv_layer):
    b, s, _ = x.shape
    tok = lambda i, j: (i, j, 0)
    const2 = lambda i, j: (0, 0)
    per_batch = lambda i, j: (i, 0, 0)
    in_specs = [
        pl.BlockSpec((1, TM, D_MODEL), tok),
        _resident((1, D_MODEL), const2),
        _resident((D_MODEL, IN_WIDTH), const2),
    ]
    args = [x, g_mix.reshape(1, D_MODEL), w_in]
    if conv_layer:
        in_specs.append(_resident((CONV_WIDTH, MIX_WIDTH), const2))
        args.append(w_conv)
    in_specs += [
        _resident((1, MEM_HEAD_DIM), const2),
        pl.BlockSpec((1, MEM_WIDTH, N_MEM), per_batch),
        pl.BlockSpec((1, N_MEM, MEM_WIDTH), per_batch),
    ]
    args += [g_q.reshape(1, MEM_HEAD_DIM), knt, v]
    mix_width = MIX_WIDTH if conv_layer else 3 * MIX_WIDTH
    return pl.pallas_call(
        _in_conv_kernel if conv_layer else _in_sb_kernel,
        grid=(b, s // TM),
        in_specs=in_specs,
        out_specs=[
            pl.BlockSpec((1, TM, mix_width), tok),
            pl.BlockSpec((1, TM, MEM_WIDTH), tok),
        ],
        out_shape=[
            jax.ShapeDtypeStruct((b, s, mix_width), BF16),
            jax.ShapeDtypeStruct((b, s, MEM_WIDTH), BF16),
        ],
        scratch_shapes=[pltpu.VMEM((SUBLANES + TM, MIX_WIDTH), F32)] if conv_layer else [],
        compiler_params=pltpu.CompilerParams(
            dimension_semantics=("arbitrary", "arbitrary"), vmem_limit_bytes=VMEM_LIMIT),
        name="in_conv" if conv_layer else "in_sb",
    )(*args)


def _sb_kernel(q_ref, k_ref, v_ref, o_ref, acc_ref, c_ref):
    qi = pl.program_id(2)
    lane = lax.broadcasted_iota(jnp.int32, (1, LANES), 1)
    head_masks = [(lane >= hh * SB_HEAD_DIM) & (lane < (hh + 1) * SB_HEAD_DIM)
                  for hh in range(HEADS_PER_COL)]
    n_heads = SB_COLS * HEADS_PER_COL
    q_all = q_ref[0]
    q_heads = []
    for cg in range(SB_COLS):
        qc = q_all[:, cg * LANES:(cg + 1) * LANES]
        q_heads += [jnp.where(m, qc, jnp.zeros_like(qc)) for m in head_masks]
    tri = (lax.broadcasted_iota(jnp.int32, (TK, TK), 0)
           >= lax.broadcasted_iota(jnp.int32, (TK, TK), 1)).astype(BF16)
    tri2 = jnp.concatenate([tri, tri], axis=0)

    acc_ref[...] = jnp.zeros_like(acc_ref)
    c_ref[...] = jnp.zeros_like(c_ref)

    def sweep_block(start, mask):
        kb = k_ref[0, pl.ds(start, TK), :]
        vb = v_ref[0, pl.ds(start, TK), :]
        k_cols = [kb[:, cg * LANES:(cg + 1) * LANES] for cg in range(SB_COLS)]
        zs = [_dot_nt(q_heads[h], k_cols[h // HEADS_PER_COL]) for h in range(n_heads)]
        suffixes = []
        for z in zs:
            sp = jnp.maximum(z, jnp.log2(1.0 + jnp.exp2(jnp.minimum(z, 126.0))))
            if mask is not None:
                sp = jnp.where(mask, sp, 0.0)
            sp_hi = sp.astype(BF16)
            sp_lo = (sp - sp_hi.astype(F32)).astype(BF16)
            suffixes.append(_dot(jnp.concatenate([sp_hi, sp_lo], axis=1), tri2))
        weights = []
        c_min = None
        for h in range(n_heads):
            a = jnp.exp2(zs[h] - suffixes[h] - c_ref[h])
            if mask is not None:
                a = jnp.where(mask, a, 0.0)
            weights.append(a.astype(BF16))
            c_new = c_ref[h] + suffixes[h][:, 0:1]
            c_ref[h] = c_new
            c_min = c_new if c_min is None else jnp.minimum(c_min, c_new)
        for cg in range(SB_COLS):
            v2 = vb[:, cg * LANES:(cg + 1) * LANES]
            v_heads = jnp.concatenate([jnp.where(m, v2, jnp.zeros_like(v2)) for m in head_masks], axis=0)
            w_cat = jnp.concatenate(weights[cg * HEADS_PER_COL:(cg + 1) * HEADS_PER_COL], axis=1)
            acc_ref[:, cg * LANES:(cg + 1) * LANES] += _dot(w_cat, v_heads)
        return jnp.min(c_min)

    n_diag = TQ // TK
    row = lax.broadcasted_iota(jnp.int32, (TQ, TK), 0)
    col = lax.broadcasted_iota(jnp.int32, (TQ, TK), 1)
    c_floor = None
    for d in range(n_diag):
        off = (n_diag - 1 - d) * TK
        c_floor = sweep_block(pl.multiple_of(qi * TQ + off, TK), col + off < row)

    n_prev = qi * n_diag

    def more(carry):
        i, c_low = carry
        return (i < n_prev) & (c_low < SKIP_LOG2)

    def body(carry):
        i, _ = carry
        return i + 1, sweep_block(pl.multiple_of((n_prev - 1 - i) * TK, TK), None)

    lax.while_loop(more, body, (jnp.int32(0), c_floor))
    o_ref[0] = acc_ref[...].astype(BF16)


def _stick_breaking(qkv):
    b, s, _ = qkv.shape
    width = SB_COLS * LANES
    n_col = MIX_WIDTH // width
    return pl.pallas_call(
        _sb_kernel,
        grid=(b, n_col, s // TQ),
        in_specs=[
            pl.BlockSpec((1, TQ, width), lambda i, p, j: (i, j, p)),
            pl.BlockSpec((1, s, width), lambda i, p, j: (i, 0, n_col + p)),
            pl.BlockSpec((1, s, width), lambda i, p, j: (i, 0, 2 * n_col + p)),
        ],
        out_specs=pl.BlockSpec((1, TQ, width), lambda i, p, j: (i, j, p)),
        out_shape=jax.ShapeDtypeStruct((b, s, MIX_WIDTH), BF16),
        scratch_shapes=[
            pltpu.VMEM((TQ, width), F32),
            pltpu.VMEM((SB_COLS * HEADS_PER_COL, TQ, 1), F32),
        ],
        compiler_params=pltpu.CompilerParams(
            dimension_semantics=("arbitrary", "arbitrary", "arbitrary"),
            vmem_limit_bytes=VMEM_LIMIT),
        name="stick_breaking",
    )(qkv, qkv, qkv)


def _out_ffn_kernel(x_ref, mix_ref, mo_ref, wo_ref, g_ref, wi_ref, wf_ref, out_ref):
    x1 = (x_ref[0] + _dot(mix_ref[0], wo_ref[0:MIX_WIDTH, :])
          + _dot(mo_ref[0], wo_ref[MIX_WIDTH:OUT_WIDTH, :]))
    h = (_rms(x1) * g_ref[...]).astype(BF16)
    ffn = None
    chunk = D_FF // FFN_CHUNKS
    for c in range(FFN_CHUNKS):
        lo = c * chunk
        gate = _dot(h, wi_ref[:, lo:lo + chunk])
        up = _dot(h, wi_ref[:, D_FF + lo:D_FF + lo + chunk])
        act = (gate * jax.nn.sigmoid(gate) * up).astype(BF16)
        part = _dot(act, wf_ref[lo:lo + chunk, :])
        ffn = part if ffn is None else ffn + part
    out_ref[0] = ffn + x1


def _out_ffn(x, mixed, mem_out, w_out, g_ffn, w_ffn_in, w_ffn_out):
    b, s, _ = x.shape
    tok = lambda i, j: (i, j, 0)
    const2 = lambda i, j: (0, 0)
    return pl.pallas_call(
        _out_ffn_kernel,
        grid=(b, s // TM),
        in_specs=[
            pl.BlockSpec((1, TM, D_MODEL), tok),
            pl.BlockSpec((1, TM, MIX_WIDTH), tok),
            pl.BlockSpec((1, TM, MEM_WIDTH), tok),
            _resident((OUT_WIDTH, D_MODEL), const2),
            _resident((1, D_MODEL), const2),
            _resident((D_MODEL, 2 * D_FF), const2),
            _resident((D_FF, D_MODEL), const2),
        ],
        out_specs=pl.BlockSpec((1, TM, D_MODEL), tok),
        out_shape=jax.ShapeDtypeStruct((b, s, D_MODEL), F32),
        compiler_params=pltpu.CompilerParams(
            dimension_semantics=("arbitrary", "arbitrary"), vmem_limit_bytes=VMEM_LIMIT),
        name="out_ffn",
    )(x, mixed, mem_out, w_out, g_ffn.reshape(1, D_MODEL), w_ffn_in, w_ffn_out)


def kernel(x, mem, w_in, w_out, w_mem_kv, w_conv, w_ffn_in, w_ffn_out, g_mix, g_mem, g_ffn, g_q, g_k):
    assert x.shape[1] % TM == 0 and x.shape[1] % TQ == 0 and TQ % TK == 0
    assert D_FF % (FFN_CHUNKS * LANES) == 0 and MIX_WIDTH % (SB_COLS * LANES) == 0
    w_in, w_out, w_mem_kv, w_ffn_in, w_ffn_out = (
        w.astype(BF16) for w in (w_in, w_out, w_mem_kv, w_ffn_in, w_ffn_out))
    knt, v = _mem_kv(mem, g_mem, w_mem_kv, g_k)
    for i in range(DEPTH):
        conv_layer = i % N_MIXERS == 0
        mixed, mem_out = _in_proj(x, g_mix[i], w_in[i], w_conv[i // N_MIXERS] if conv_layer else None,
                                  g_q[i], knt[i], v[i], conv_layer)
        if not conv_layer:
            mixed = _stick_breaking(mixed)
        x = _out_ffn(x, mixed, mem_out, w_out[i], g_ffn[i], w_ffn_in[i], w_ffn_out[i])
    return x
```

```python
import math

import jax
import jax.numpy as jnp
from jax import lax
from jax.experimental import pallas as pl
from jax.experimental.pallas import tpu as pltpu

D_MODEL = 1024
DEPTH = 4
N_MIXERS = 2
CONV_WIDTH = 3
MIX_WIDTH = D_MODEL
SB_HEADS = 16
SB_HEAD_DIM = MIX_WIDTH // SB_HEADS
N_MEM = 256
MEM_HEADS = 4
MEM_HEAD_DIM = 128
MEM_WIDTH = MEM_HEADS * MEM_HEAD_DIM
IN_WIDTH = 3 * MIX_WIDTH + MEM_WIDTH
OUT_WIDTH = MIX_WIDTH + MEM_WIDTH
D_FF = -(-(8 * D_MODEL) // (3 * 256)) * 256
EPS = 1e-6

LANES = 128
SUBLANES = 8
VMEM_LIMIT = 56 * 1024 * 1024

TM = 512
TQ = 256
TK = 256
FFN_CHUNKS = 2
HEADS_PER_COL = LANES // SB_HEAD_DIM
SB_COLS = 2
LOG2E = math.log2(math.e)
SKIP_LOG2 = 160.0

BF16 = jnp.bfloat16
F32 = jnp.float32


def _dot(a, b):
    return jnp.dot(a, b, preferred_element_type=F32)


def _dot_nt(a, b):
    return lax.dot_general(a, b, (((1,), (1,)), ((), ())), preferred_element_type=F32)


def _rms(x):
    return x * lax.rsqrt(jnp.mean(x * x, axis=-1, keepdims=True) + EPS)


def _layer_param(layer, *block_shape):
    zeros = (0,) * len(block_shape)
    return pl.BlockSpec((None,) + block_shape, lambda i, j: (layer,) + zeros,
                        pipeline_mode=pl.Buffered(1))


def _mem_kv_kernel(mem_ref, g_ref, w_ref, gk_ref, knt_ref, v_ref):
    h = (_rms(mem_ref[0]) * g_ref[0]).astype(BF16)
    kv = _dot(h, w_ref[0])
    gk = gk_ref[0]
    for hd in range(MEM_HEADS):
        lo = hd * MEM_HEAD_DIM
        kn = _rms(kv[:, lo:lo + MEM_HEAD_DIM]) * gk
        knt_ref[0, 0, lo:lo + MEM_HEAD_DIM, :] = kn.T.astype(BF16)
    v_ref[0, 0] = kv[:, MEM_WIDTH:].astype(BF16)


def _mem_kv(mem, g_mem, w_mem_kv, g_k):
    b = mem.shape[0]
    return pl.pallas_call(
        _mem_kv_kernel,
        grid=(DEPTH, b),
        in_specs=[
            pl.BlockSpec((1, N_MEM, D_MODEL), lambda d, i: (i, 0, 0)),
            pl.BlockSpec((1, 1, D_MODEL), lambda d, i: (d, 0, 0)),
            pl.BlockSpec((1, D_MODEL, 2 * MEM_WIDTH), lambda d, i: (d, 0, 0)),
            pl.BlockSpec((1, 1, MEM_HEAD_DIM), lambda d, i: (d, 0, 0)),
        ],
        out_specs=[
            pl.BlockSpec((1, 1, MEM_WIDTH, N_MEM), lambda d, i: (d, i, 0, 0)),
            pl.BlockSpec((1, 1, N_MEM, MEM_WIDTH), lambda d, i: (d, i, 0, 0)),
        ],
        out_shape=[
            jax.ShapeDtypeStruct((DEPTH, b, MEM_WIDTH, N_MEM), BF16),
            jax.ShapeDtypeStruct((DEPTH, b, N_MEM, MEM_WIDTH), BF16),
        ],
        compiler_params=pltpu.CompilerParams(
            dimension_semantics=("arbitrary", "arbitrary"), vmem_limit_bytes=VMEM_LIMIT),
        name="mem_kv",
    )(mem, g_mem.reshape(DEPTH, 1, D_MODEL), w_mem_kv, g_k.reshape(DEPTH, 1, MEM_HEAD_DIM))


def _mem_attention(h, w_ref, gq_ref, knt_ref, v_ref, out_ref):
    qm = _dot(h, w_ref[:, 3 * MIX_WIDTH:])
    gq = gq_ref[...]
    for hd in range(MEM_HEADS):
        lo = hd * MEM_HEAD_DIM
        qn = (_rms(qm[:, lo:lo + MEM_HEAD_DIM]) * gq).astype(BF16)
        s = _dot(qn, knt_ref[0, lo:lo + MEM_HEAD_DIM, :]) * (MEM_HEAD_DIM ** -0.5)
        e = jnp.exp(s - jnp.max(s, axis=-1, keepdims=True))
        inv = 1.0 / jnp.sum(e, axis=-1, keepdims=True)
        o = _dot(e.astype(BF16), v_ref[0, :, lo:lo + MEM_HEAD_DIM]) * inv
        out_ref[0, :, lo:lo + MEM_HEAD_DIM] = o.astype(BF16)


def _in_conv_kernel(x_ref, g_ref, w_ref, wc_ref, gq_ref, knt_ref, v_ref, mix_ref, mo_ref, ybuf):
    @pl.when(pl.program_id(1) == 0)
    def _():
        ybuf[0:SUBLANES, :] = jnp.zeros((SUBLANES, MIX_WIDTH), F32)

    h = (_rms(x_ref[0]) * g_ref[...]).astype(BF16)
    y = _dot(h, w_ref[:, MIX_WIDTH:2 * MIX_WIDTH]) * _dot(h, w_ref[:, 2 * MIX_WIDTH:3 * MIX_WIDTH])
    ybuf[SUBLANES:SUBLANES + TM, :] = y
    conv = (wc_ref[2:3, :] * y
            + wc_ref[1:2, :] * ybuf[SUBLANES - 1:SUBLANES - 1 + TM, :]
            + wc_ref[0:1, :] * ybuf[SUBLANES - 2:SUBLANES - 2 + TM, :])
    ybuf[0:SUBLANES, :] = ybuf[TM:TM + SUBLANES, :]
    mix_ref[0] = (_dot(h, w_ref[:, 0:MIX_WIDTH]) * conv).astype(BF16)
    _mem_attention(h, w_ref, gq_ref, knt_ref, v_ref, mo_ref)


def _in_sb_kernel(x_ref, g_ref, w_ref, gq_ref, knt_ref, v_ref, qkv_ref, mo_ref):
    h = (_rms(x_ref[0]) * g_ref[...]).astype(BF16)
    qkv_ref[0, :, 0:MIX_WIDTH] = (
        _dot(h, w_ref[:, 0:MIX_WIDTH]) * (LOG2E * SB_HEAD_DIM ** -0.5)).astype(BF16)
    qkv_ref[0, :, MIX_WIDTH:3 * MIX_WIDTH] = _dot(h, w_ref[:, MIX_WIDTH:3 * MIX_WIDTH]).astype(BF16)
    _mem_attention(h, w_ref, gq_ref, knt_ref, v_ref, mo_ref)


def _in_proj(layer, x, g_mix, w_in, w_conv, g_q, knt, v):
    b, s, _ = x.shape
    conv_layer = layer % N_MIXERS == 0
    tok = lambda i, j: (i, j, 0)
    in_specs = [
        pl.BlockSpec((1, TM, D_MODEL), tok),
        _layer_param(layer, 1, D_MODEL),
        _layer_param(layer, D_MODEL, IN_WIDTH),
    ]
    args = [x, g_mix.reshape(DEPTH, 1, D_MODEL), w_in]
    if conv_layer:
        in_specs.append(_layer_param(layer // N_MIXERS, CONV_WIDTH, MIX_WIDTH))
        args.append(w_conv)
    in_specs += [
        _layer_param(layer, 1, MEM_HEAD_DIM),
        pl.BlockSpec((None, 1, MEM_WIDTH, N_MEM), lambda i, j: (layer, i, 0, 0)),
        pl.BlockSpec((None, 1, N_MEM, MEM_WIDTH), lambda i, j: (layer, i, 0, 0)),
    ]
    args += [g_q.reshape(DEPTH, 1, MEM_HEAD_DIM), knt, v]
    mix_width = MIX_WIDTH if conv_layer else 3 * MIX_WIDTH
    return pl.pallas_call(
        _in_conv_kernel if conv_layer else _in_sb_kernel,
        grid=(b, s // TM),
        in_specs=in_specs,
        out_specs=[
            pl.BlockSpec((1, TM, mix_width), tok),
            pl.BlockSpec((1, TM, MEM_WIDTH), tok),
        ],
        out_shape=[
            jax.ShapeDtypeStruct((b, s, mix_width), BF16),
            jax.ShapeDtypeStruct((b, s, MEM_WIDTH), BF16),
        ],
        scratch_shapes=[pltpu.VMEM((SUBLANES + TM, MIX_WIDTH), F32)] if conv_layer else [],
        compiler_params=pltpu.CompilerParams(
            dimension_semantics=("arbitrary", "arbitrary"), vmem_limit_bytes=VMEM_LIMIT),
        name="in_conv" if conv_layer else "in_sb",
    )(*args)


def _sb_kernel(q_ref, k_ref, v_ref, o_ref, acc_ref, c_ref):
    qi = pl.program_id(2)
    lane = lax.broadcasted_iota(jnp.int32, (1, LANES), 1)
    head_masks = [(lane >= hh * SB_HEAD_DIM) & (lane < (hh + 1) * SB_HEAD_DIM)
                  for hh in range(HEADS_PER_COL)]
    n_heads = SB_COLS * HEADS_PER_COL
    q_all = q_ref[0]
    q_heads = []
    for cg in range(SB_COLS):
        qc = q_all[:, cg * LANES:(cg + 1) * LANES]
        q_heads += [jnp.where(m, qc, jnp.zeros_like(qc)) for m in head_masks]
    tri = (lax.broadcasted_iota(jnp.int32, (TK, TK), 0)
           >= lax.broadcasted_iota(jnp.int32, (TK, TK), 1)).astype(BF16)

    def sweep_block(start, mask, carries):
        kb = k_ref[0, pl.ds(start, TK), :]
        vb = v_ref[0, pl.ds(start, TK), :]
        k_cols = [kb[:, cg * LANES:(cg + 1) * LANES] for cg in range(SB_COLS)]
        zs = [_dot_nt(q_heads[h], k_cols[h // HEADS_PER_COL]) for h in range(n_heads)]
        suffixes = []
        for z in zs:
            sp = jnp.maximum(z, jnp.log2(1.0 + jnp.exp2(jnp.minimum(z, 126.0))))
            if mask is not None:
                sp = jnp.where(mask, sp, 0.0)
            suffixes.append(_dot(sp.astype(BF16), tri))
        weights = []
        for h in range(n_heads):
            a = jnp.exp2(zs[h] - suffixes[h] - carries[h])
            if mask is not None:
                a = jnp.where(mask, a, 0.0)
            weights.append(a.astype(BF16))
        carries = [carries[h] + suffixes[h][:, 0:1] for h in range(n_heads)]
        outs = []
        for cg in range(SB_COLS):
            v2 = vb[:, cg * LANES:(cg + 1) * LANES]
            v_heads = jnp.concatenate([jnp.where(m, v2, jnp.zeros_like(v2)) for m in head_masks], axis=0)
            w_cat = jnp.concatenate(weights[cg * HEADS_PER_COL:(cg + 1) * HEADS_PER_COL], axis=1)
            outs.append(_dot(w_cat, v_heads))
        return outs, carries

    def lowest(carries):
        low = carries[0]
        for c in carries[1:]:
            low = jnp.minimum(low, c)
        return jnp.min(low)

    def finish(outs, carries):
        acc_ref[...] = jnp.concatenate(outs, axis=1)
        for h in range(n_heads):
            c_ref[h] = carries[h]
        return lowest(carries)

    n_diag = TQ // TK
    n_prev = qi * n_diag
    row = lax.broadcasted_iota(jnp.int32, (TQ, TK), 0)
    col = lax.broadcasted_iota(jnp.int32, (TQ, TK), 1)

    def diagonal():
        outs = [jnp.zeros((TQ, LANES), F32)] * SB_COLS
        carries = [jnp.zeros((TQ, 1), F32)] * n_heads
        for d in range(n_diag):
            off = (n_diag - 1 - d) * TK
            part, carries = sweep_block(pl.multiple_of(qi * TQ + off, TK), col + off < row, carries)
            outs = [o + p for o, p in zip(outs, part)]
        return outs, carries

    def first_query_block():
        return finish(*diagonal())

    def later_query_block():
        outs, carries = diagonal()
        part, carries = sweep_block(pl.multiple_of((n_prev - 1) * TK, TK), None, carries)
        return finish([o + p for o, p in zip(outs, part)], carries)

    c_floor = lax.cond(qi > 0, later_query_block, first_query_block)

    def more(carry):
        i, c_low = carry
        return (i < n_prev) & (c_low < SKIP_LOG2)

    def body(carry):
        i, _ = carry
        part, carries = sweep_block(pl.multiple_of((n_prev - 1 - i) * TK, TK), None,
                                    [c_ref[h] for h in range(n_heads)])
        for cg in range(SB_COLS):
            acc_ref[:, cg * LANES:(cg + 1) * LANES] += part[cg]
        for h in range(n_heads):
            c_ref[h] = carries[h]
        return i + 1, lowest(carries)

    lax.while_loop(more, body, (jnp.int32(1), c_floor))
    o_ref[0] = acc_ref[...].astype(BF16)


def _stick_breaking(qkv):
    b, s, _ = qkv.shape
    width = SB_COLS * LANES
    n_col = MIX_WIDTH // width
    return pl.pallas_call(
        _sb_kernel,
        grid=(b, n_col, s // TQ),
        in_specs=[
            pl.BlockSpec((1, TQ, width), lambda i, p, j: (i, j, p)),
            pl.BlockSpec((1, s, width), lambda i, p, j: (i, 0, n_col + p)),
            pl.BlockSpec((1, s, width), lambda i, p, j: (i, 0, 2 * n_col + p)),
        ],
        out_specs=pl.BlockSpec((1, TQ, width), lambda i, p, j: (i, j, p)),
        out_shape=jax.ShapeDtypeStruct((b, s, MIX_WIDTH), BF16),
        scratch_shapes=[
            pltpu.VMEM((TQ, width), F32),
            pltpu.VMEM((SB_COLS * HEADS_PER_COL, TQ, 1), F32),
        ],
        compiler_params=pltpu.CompilerParams(
            dimension_semantics=("arbitrary", "arbitrary", "arbitrary"),
            vmem_limit_bytes=VMEM_LIMIT),
        name="stick_breaking",
    )(qkv, qkv, qkv)


def _out_ffn_kernel(x_ref, mix_ref, mo_ref, wo_ref, g_ref, wi_ref, wf_ref, out_ref):
    x1 = (x_ref[0] + _dot(mix_ref[0], wo_ref[0:MIX_WIDTH, :])
          + _dot(mo_ref[0], wo_ref[MIX_WIDTH:OUT_WIDTH, :]))
    h = (_rms(x1) * g_ref[...]).astype(BF16)
    ffn = None
    chunk = D_FF // FFN_CHUNKS
    for c in range(FFN_CHUNKS):
        lo = c * chunk
        gate = _dot(h, wi_ref[:, lo:lo + chunk])
        up = _dot(h, wi_ref[:, D_FF + lo:D_FF + lo + chunk])
        act = (gate * jax.nn.sigmoid(gate) * up).astype(BF16)
        part = _dot(act, wf_ref[lo:lo + chunk, :])
        ffn = part if ffn is None else ffn + part
    out_ref[0] = ffn + x1


def _out_ffn(layer, x, mixed, mem_out, w_out, g_ffn, w_ffn_in, w_ffn_out):
    b, s, _ = x.shape
    tok = lambda i, j: (i, j, 0)
    return pl.pallas_call(
        _out_ffn_kernel,
        grid=(b, s // TM),
        in_specs=[
            pl.BlockSpec((1, TM, D_MODEL), tok),
            pl.BlockSpec((1, TM, MIX_WIDTH), tok),
            pl.BlockSpec((1, TM, MEM_WIDTH), tok),
            _layer_param(layer, OUT_WIDTH, D_MODEL),
            _layer_param(layer, 1, D_MODEL),
            _layer_param(layer, D_MODEL, 2 * D_FF),
            _layer_param(layer, D_FF, D_MODEL),
        ],
        out_specs=pl.BlockSpec((1, TM, D_MODEL), tok),
        out_shape=jax.ShapeDtypeStruct((b, s, D_MODEL), F32),
        compiler_params=pltpu.CompilerParams(
            dimension_semantics=("arbitrary", "arbitrary"), vmem_limit_bytes=VMEM_LIMIT),
        name="out_ffn",
    )(x, mixed, mem_out, w_out, g_ffn.reshape(DEPTH, 1, D_MODEL), w_ffn_in, w_ffn_out)


def kernel(x, mem, w_in, w_out, w_mem_kv, w_conv, w_ffn_in, w_ffn_out, g_mix, g_mem, g_ffn, g_q, g_k):
    assert x.shape[1] % TM == 0 and x.shape[1] % TQ == 0 and TQ % TK == 0
    assert D_FF % (FFN_CHUNKS * LANES) == 0 and MIX_WIDTH % (SB_COLS * LANES) == 0
    w_in, w_out, w_mem_kv, w_ffn_in, w_ffn_out = (
        w.astype(BF16) for w in (w_in, w_out, w_mem_kv, w_ffn_in, w_ffn_out))
    knt, v = _mem_kv(mem, g_mem, w_mem_kv, g_k)
    for layer in range(DEPTH):
        mixed, mem_out = _in_proj(layer, x, g_mix, w_in, w_conv, g_q, knt, v)
        if layer % N_MIXERS == 1:
            mixed = _stick_breaking(mixed)
        x = _out_ffn(layer, x, mixed, mem_out, w_out, g_ffn, w_ffn_in, w_ffn_out)
    return x
```

```python
import math

import jax
import jax.numpy as jnp
from jax import lax
from jax.experimental import pallas as pl
from jax.experimental.pallas import tpu as pltpu

D_MODEL = 1024
DEPTH = 4
N_MIXERS = 2
CONV_WIDTH = 3
MIX_WIDTH = D_MODEL
SB_HEADS = 16
SB_HEAD_DIM = MIX_WIDTH // SB_HEADS
N_MEM = 256
MEM_HEADS = 4
MEM_HEAD_DIM = 128
MEM_WIDTH = MEM_HEADS * MEM_HEAD_DIM
IN_WIDTH = 3 * MIX_WIDTH + MEM_WIDTH
OUT_WIDTH = MIX_WIDTH + MEM_WIDTH
D_FF = -(-(8 * D_MODEL) // (3 * 256)) * 256
EPS = 1e-6

LANES = 128
SUBLANES = 8
VMEM_LIMIT = 56 * 1024 * 1024

TM = 512
SUBTILES = 2
TILE = SUBTILES * TM
TQ = 256
TK = 256
MXU_WIDTH = 256
FFN_SPLITS = ((0, 6 * MXU_WIDTH), (6 * MXU_WIDTH, D_FF))
HEADS_PER_COL = LANES // SB_HEAD_DIM
SB_COLS = 2
LOG2E = math.log2(math.e)
SKIP_LOG2 = 160.0

BF16 = jnp.bfloat16
F32 = jnp.float32


def _dot(a, b):
    return jnp.dot(a, b, preferred_element_type=F32)


def _dot_nt(a, b):
    return lax.dot_general(a, b, (((1,), (1,)), ((), ())), preferred_element_type=F32)


def _rms(x):
    return x * lax.rsqrt(jnp.mean(x * x, axis=-1, keepdims=True) + EPS)


def _layer_param(layer, *block_shape):
    zeros = (0,) * len(block_shape)
    return pl.BlockSpec((None,) + block_shape, lambda i, j: (layer,) + zeros,
                        pipeline_mode=pl.Buffered(1))


def _mem_kv_kernel(mem_ref, g_ref, w_ref, gk_ref, knt_ref, v_ref):
    h = (_rms(mem_ref[0]) * g_ref[0]).astype(BF16)
    kv = _dot(h, w_ref[0])
    gk = gk_ref[0]
    for hd in range(MEM_HEADS):
        lo = hd * MEM_HEAD_DIM
        kn = _rms(kv[:, lo:lo + MEM_HEAD_DIM]) * gk
        knt_ref[0, 0, lo:lo + MEM_HEAD_DIM, :] = kn.T.astype(BF16)
    v_ref[0, 0] = kv[:, MEM_WIDTH:].astype(BF16)


def _mem_kv(mem, g_mem, w_mem_kv, g_k):
    b = mem.shape[0]
    return pl.pallas_call(
        _mem_kv_kernel,
        grid=(DEPTH, b),
        in_specs=[
            pl.BlockSpec((1, N_MEM, D_MODEL), lambda d, i: (i, 0, 0)),
            pl.BlockSpec((1, 1, D_MODEL), lambda d, i: (d, 0, 0)),
            pl.BlockSpec((1, D_MODEL, 2 * MEM_WIDTH), lambda d, i: (d, 0, 0)),
            pl.BlockSpec((1, 1, MEM_HEAD_DIM), lambda d, i: (d, 0, 0)),
        ],
        out_specs=[
            pl.BlockSpec((1, 1, MEM_WIDTH, N_MEM), lambda d, i: (d, i, 0, 0)),
            pl.BlockSpec((1, 1, N_MEM, MEM_WIDTH), lambda d, i: (d, i, 0, 0)),
        ],
        out_shape=[
            jax.ShapeDtypeStruct((DEPTH, b, MEM_WIDTH, N_MEM), BF16),
            jax.ShapeDtypeStruct((DEPTH, b, N_MEM, MEM_WIDTH), BF16),
        ],
        compiler_params=pltpu.CompilerParams(
            dimension_semantics=("arbitrary", "arbitrary"), vmem_limit_bytes=VMEM_LIMIT),
        name="mem_kv",
    )(mem, g_mem.reshape(DEPTH, 1, D_MODEL), w_mem_kv, g_k.reshape(DEPTH, 1, MEM_HEAD_DIM))


def _mem_attention(h, w_ref, gq_ref, knt_ref, v_ref, out_ref, rows):
    qm = _dot(h, w_ref[:, 3 * MIX_WIDTH:])
    gq = gq_ref[...]
    for hd in range(MEM_HEADS):
        lo = hd * MEM_HEAD_DIM
        qn = (_rms(qm[:, lo:lo + MEM_HEAD_DIM]) * gq).astype(BF16)
        s = _dot(qn, knt_ref[0, lo:lo + MEM_HEAD_DIM, :]) * (MEM_HEAD_DIM ** -0.5)
        e = jnp.exp(s - jnp.max(s, axis=-1, keepdims=True))
        inv = 1.0 / jnp.sum(e, axis=-1, keepdims=True)
        o = _dot(e.astype(BF16), v_ref[0, :, lo:lo + MEM_HEAD_DIM]) * inv
        out_ref[0, rows, lo:lo + MEM_HEAD_DIM] = o.astype(BF16)


def _in_conv_kernel(x_ref, g_ref, w_ref, wc_ref, gq_ref, knt_ref, v_ref, mix_ref, mo_ref, ybuf):
    @pl.when(pl.program_id(1) == 0)
    def _():
        ybuf[0:SUBLANES, :] = jnp.zeros((SUBLANES, MIX_WIDTH), F32)

    for sub in range(SUBTILES):
        rows = pl.ds(sub * TM, TM)
        h = (_rms(x_ref[0, rows, :]) * g_ref[...]).astype(BF16)
        y = _dot(h, w_ref[:, MIX_WIDTH:2 * MIX_WIDTH]) * _dot(h, w_ref[:, 2 * MIX_WIDTH:3 * MIX_WIDTH])
        ybuf[SUBLANES:SUBLANES + TM, :] = y
        conv = (wc_ref[2:3, :] * y
                + wc_ref[1:2, :] * ybuf[SUBLANES - 1:SUBLANES - 1 + TM, :]
                + wc_ref[0:1, :] * ybuf[SUBLANES - 2:SUBLANES - 2 + TM, :])
        ybuf[0:SUBLANES, :] = ybuf[TM:TM + SUBLANES, :]
        mix_ref[0, rows, :] = (_dot(h, w_ref[:, 0:MIX_WIDTH]) * conv).astype(BF16)
        _mem_attention(h, w_ref, gq_ref, knt_ref, v_ref, mo_ref, rows)


def _in_sb_kernel(x_ref, g_ref, w_ref, gq_ref, knt_ref, v_ref, qkv_ref, mo_ref):
    for sub in range(SUBTILES):
        rows = pl.ds(sub * TM, TM)
        h = (_rms(x_ref[0, rows, :]) * g_ref[...]).astype(BF16)
        qkv_ref[0, rows, 0:MIX_WIDTH] = (
            _dot(h, w_ref[:, 0:MIX_WIDTH]) * (LOG2E * SB_HEAD_DIM ** -0.5)).astype(BF16)
        qkv_ref[0, rows, MIX_WIDTH:3 * MIX_WIDTH] = (
            _dot(h, w_ref[:, MIX_WIDTH:3 * MIX_WIDTH]).astype(BF16))
        _mem_attention(h, w_ref, gq_ref, knt_ref, v_ref, mo_ref, rows)


def _in_proj(layer, x, g_mix, w_in, w_conv, g_q, knt, v):
    b, s, _ = x.shape
    conv_layer = layer % N_MIXERS == 0
    tok = lambda i, j: (i, j, 0)
    in_specs = [
        pl.BlockSpec((1, TILE, D_MODEL), tok),
        _layer_param(layer, 1, D_MODEL),
        _layer_param(layer, D_MODEL, IN_WIDTH),
    ]
    args = [x, g_mix.reshape(DEPTH, 1, D_MODEL), w_in]
    if conv_layer:
        in_specs.append(_layer_param(layer // N_MIXERS, CONV_WIDTH, MIX_WIDTH))
        args.append(w_conv)
    in_specs += [
        _layer_param(layer, 1, MEM_HEAD_DIM),
        pl.BlockSpec((None, 1, MEM_WIDTH, N_MEM), lambda i, j: (layer, i, 0, 0)),
        pl.BlockSpec((None, 1, N_MEM, MEM_WIDTH), lambda i, j: (layer, i, 0, 0)),
    ]
    args += [g_q.reshape(DEPTH, 1, MEM_HEAD_DIM), knt, v]
    mix_width = MIX_WIDTH if conv_layer else 3 * MIX_WIDTH
    return pl.pallas_call(
        _in_conv_kernel if conv_layer else _in_sb_kernel,
        grid=(b, s // TILE),
        in_specs=in_specs,
        out_specs=[
            pl.BlockSpec((1, TILE, mix_width), tok),
            pl.BlockSpec((1, TILE, MEM_WIDTH), tok),
        ],
        out_shape=[
            jax.ShapeDtypeStruct((b, s, mix_width), BF16),
            jax.ShapeDtypeStruct((b, s, MEM_WIDTH), BF16),
        ],
        scratch_shapes=[pltpu.VMEM((SUBLANES + TM, MIX_WIDTH), F32)] if conv_layer else [],
        compiler_params=pltpu.CompilerParams(
            dimension_semantics=("arbitrary", "arbitrary"), vmem_limit_bytes=VMEM_LIMIT),
        name="in_conv" if conv_layer else "in_sb",
    )(*args)


def _sb_kernel(q_ref, k_ref, v_ref, o_ref, acc_ref, c_ref):
    qi = pl.program_id(2)
    lane = lax.broadcasted_iota(jnp.int32, (1, LANES), 1)
    head_masks = [(lane >= hh * SB_HEAD_DIM) & (lane < (hh + 1) * SB_HEAD_DIM)
                  for hh in range(HEADS_PER_COL)]
    n_heads = SB_COLS * HEADS_PER_COL
    q_all = q_ref[0]
    q_heads = []
    for cg in range(SB_COLS):
        qc = q_all[:, cg * LANES:(cg + 1) * LANES]
        q_heads += [jnp.where(m, qc, jnp.zeros_like(qc)) for m in head_masks]
    tri = (lax.broadcasted_iota(jnp.int32, (TK, TK), 0)
           >= lax.broadcasted_iota(jnp.int32, (TK, TK), 1)).astype(BF16)

    def sweep_block(start, mask, carries):
        kb = k_ref[0, pl.ds(start, TK), :]
        vb = v_ref[0, pl.ds(start, TK), :]
        k_cols = [kb[:, cg * LANES:(cg + 1) * LANES] for cg in range(SB_COLS)]
        zs = [_dot_nt(q_heads[h], k_cols[h // HEADS_PER_COL]) for h in range(n_heads)]
        suffixes = []
        for z in zs:
            sp = jnp.maximum(z, jnp.log2(1.0 + jnp.exp2(jnp.minimum(z, 126.0))))
            if mask is not None:
                sp = jnp.where(mask, sp, 0.0)
            suffixes.append(_dot(sp.astype(BF16), tri))
        weights = []
        for h in range(n_heads):
            a = jnp.exp2(zs[h] - suffixes[h] - carries[h])
            if mask is not None:
                a = jnp.where(mask, a, 0.0)
            weights.append(a.astype(BF16))
        carries = [carries[h] + suffixes[h][:, 0:1] for h in range(n_heads)]
        outs = []
        for cg in range(SB_COLS):
            v2 = vb[:, cg * LANES:(cg + 1) * LANES]
            v_heads = jnp.concatenate([jnp.where(m, v2, jnp.zeros_like(v2)) for m in head_masks], axis=0)
            w_cat = jnp.concatenate(weights[cg * HEADS_PER_COL:(cg + 1) * HEADS_PER_COL], axis=1)
            outs.append(_dot(w_cat, v_heads))
        return outs, carries

    def lowest(carries):
        low = carries[0]
        for c in carries[1:]:
            low = jnp.minimum(low, c)
        return jnp.min(low)

    def finish(outs, carries):
        acc_ref[...] = jnp.concatenate(outs, axis=1)
        for h in range(n_heads):
            c_ref[h] = carries[h]
        return lowest(carries)

    n_diag = TQ // TK
    n_prev = qi * n_diag
    row = lax.broadcasted_iota(jnp.int32, (TQ, TK), 0)
    col = lax.broadcasted_iota(jnp.int32, (TQ, TK), 1)

    def diagonal():
        outs = [jnp.zeros((TQ, LANES), F32)] * SB_COLS
        carries = [jnp.zeros((TQ, 1), F32)] * n_heads
        for d in range(n_diag):
            off = (n_diag - 1 - d) * TK
            part, carries = sweep_block(pl.multiple_of(qi * TQ + off, TK), col + off < row, carries)
            outs = [o + p for o, p in zip(outs, part)]
        return outs, carries

    def first_query_block():
        return finish(*diagonal())

    def later_query_block():
        outs, carries = diagonal()
        part, carries = sweep_block(pl.multiple_of((n_prev - 1) * TK, TK), None, carries)
        return finish([o + p for o, p in zip(outs, part)], carries)

    c_floor = lax.cond(qi > 0, later_query_block, first_query_block)

    def more(carry):
        i, c_low = carry
        return (i < n_prev) & (c_low < SKIP_LOG2)

    def body(carry):
        i, _ = carry
        part, carries = sweep_block(pl.multiple_of((n_prev - 1 - i) * TK, TK), None,
                                    [c_ref[h] for h in range(n_heads)])
        for cg in range(SB_COLS):
            acc_ref[:, cg * LANES:(cg + 1) * LANES] += part[cg]
        for h in range(n_heads):
            c_ref[h] = carries[h]
        return i + 1, lowest(carries)

    lax.while_loop(more, body, (jnp.int32(1), c_floor))
    o_ref[0] = acc_ref[...].astype(BF16)


def _stick_breaking(qkv):
    b, s, _ = qkv.shape
    width = SB_COLS * LANES
    n_col = MIX_WIDTH // width
    return pl.pallas_call(
        _sb_kernel,
        grid=(b, n_col, s // TQ),
        in_specs=[
            pl.BlockSpec((1, TQ, width), lambda i, p, j: (i, j, p)),
            pl.BlockSpec((1, s, width), lambda i, p, j: (i, 0, n_col + p)),
            pl.BlockSpec((1, s, width), lambda i, p, j: (i, 0, 2 * n_col + p)),
        ],
        out_specs=pl.BlockSpec((1, TQ, width), lambda i, p, j: (i, j, p)),
        out_shape=jax.ShapeDtypeStruct((b, s, MIX_WIDTH), BF16),
        scratch_shapes=[
            pltpu.VMEM((TQ, width), F32),
            pltpu.VMEM((SB_COLS * HEADS_PER_COL, TQ, 1), F32),
        ],
        compiler_params=pltpu.CompilerParams(
            dimension_semantics=("arbitrary", "arbitrary", "arbitrary"),
            vmem_limit_bytes=VMEM_LIMIT),
        name="stick_breaking",
    )(qkv, qkv, qkv)


def _out_ffn_kernel(x_ref, mix_ref, mo_ref, wo_ref, g_ref, wi_ref, wf_ref, out_ref):
    for sub in range(SUBTILES):
        rows = pl.ds(sub * TM, TM)
        x1 = (x_ref[0, rows, :] + _dot(mix_ref[0, rows, :], wo_ref[0:MIX_WIDTH, :])
              + _dot(mo_ref[0, rows, :], wo_ref[MIX_WIDTH:OUT_WIDTH, :]))
        h = (_rms(x1) * g_ref[...]).astype(BF16)
        ffn = None
        for lo, hi in FFN_SPLITS:
            gate = _dot(h, wi_ref[:, lo:hi])
            up = _dot(h, wi_ref[:, D_FF + lo:D_FF + hi])
            act = (gate * jax.nn.sigmoid(gate) * up).astype(BF16)
            part = _dot(act, wf_ref[lo:hi, :])
            ffn = part if ffn is None else ffn + part
        out_ref[0, rows, :] = ffn + x1


def _out_ffn(layer, x, mixed, mem_out, w_out, g_ffn, w_ffn_in, w_ffn_out):
    b, s, _ = x.shape
    tok = lambda i, j: (i, j, 0)
    return pl.pallas_call(
        _out_ffn_kernel,
        grid=(b, s // TILE),
        in_specs=[
            pl.BlockSpec((1, TILE, D_MODEL), tok),
            pl.BlockSpec((1, TILE, MIX_WIDTH), tok),
            pl.BlockSpec((1, TILE, MEM_WIDTH), tok),
            _layer_param(layer, OUT_WIDTH, D_MODEL),
            _layer_param(layer, 1, D_MODEL),
            _layer_param(layer, D_MODEL, 2 * D_FF),
            _layer_param(layer, D_FF, D_MODEL),
        ],
        out_specs=pl.BlockSpec((1, TILE, D_MODEL), tok),
        out_shape=jax.ShapeDtypeStruct((b, s, D_MODEL), F32),
        compiler_params=pltpu.CompilerParams(
            dimension_semantics=("arbitrary", "arbitrary"), vmem_limit_bytes=VMEM_LIMIT),
        name="out_ffn",
    )(x, mixed, mem_out, w_out, g_ffn.reshape(DEPTH, 1, D_MODEL), w_ffn_in, w_ffn_out)


def kernel(x, mem, w_in, w_out, w_mem_kv, w_conv, w_ffn_in, w_ffn_out, g_mix, g_mem, g_ffn, g_q, g_k):
    assert x.shape[1] % TILE == 0 and x.shape[1] % TQ == 0 and TQ % TK == 0
    assert MIX_WIDTH % (SB_COLS * LANES) == 0
    w_in, w_out, w_mem_kv, w_ffn_in, w_ffn_out = (
        w.astype(BF16) for w in (w_in, w_out, w_mem_kv, w_ffn_in, w_ffn_out))
    knt, v = _mem_kv(mem, g_mem, w_mem_kv, g_k)
    for layer in range(DEPTH):
        mixed, mem_out = _in_proj(layer, x, g_mix, w_in, w_conv, g_q, knt, v)
        if layer % N_MIXERS == 1:
            mixed = _stick_breaking(mixed)
        x = _out_ffn(layer, x, mixed, mem_out, w_out, g_ffn, w_ffn_in, w_ffn_out)
    return x
```

```python
import math

import jax
import jax.numpy as jnp
from jax import lax
from jax.experimental import pallas as pl
from jax.experimental.pallas import tpu as pltpu

D_MODEL = 1024
DEPTH = 4
N_MIXERS = 2
CONV_WIDTH = 3
MIX_WIDTH = D_MODEL
SB_HEADS = 16
SB_HEAD_DIM = MIX_WIDTH // SB_HEADS
N_MEM = 256
MEM_HEADS = 4
MEM_HEAD_DIM = 128
MEM_WIDTH = MEM_HEADS * MEM_HEAD_DIM
IN_WIDTH = 3 * MIX_WIDTH + MEM_WIDTH
OUT_WIDTH = MIX_WIDTH + MEM_WIDTH
D_FF = -(-(8 * D_MODEL) // (3 * 256)) * 256
EPS = 1e-6

LANES = 128
SUBLANES = 8
VMEM_LIMIT = 56 * 1024 * 1024

TM = 512
SUBTILES = 2
TILE = SUBTILES * TM
TQ = 256
TK = 256
MXU_WIDTH = 256
FFN_SPLITS = ((0, 6 * MXU_WIDTH), (6 * MXU_WIDTH, D_FF))
HEADS_PER_COL = LANES // SB_HEAD_DIM
SB_COLS = 2
SB_SUBS = 2
LOG2E = math.log2(math.e)
SKIP_LOG2 = 160.0

BF16 = jnp.bfloat16
F32 = jnp.float32


def _dot(a, b):
    return jnp.dot(a, b, preferred_element_type=F32)


def _dot_nt(a, b):
    return lax.dot_general(a, b, (((1,), (1,)), ((), ())), preferred_element_type=F32)


def _rms(x):
    return x * lax.rsqrt(jnp.mean(x * x, axis=-1, keepdims=True) + EPS)


def _layer_param(layer, *block_shape):
    zeros = (0,) * len(block_shape)
    return pl.BlockSpec((None,) + block_shape, lambda i, j: (layer,) + zeros,
                        pipeline_mode=pl.Buffered(1))


def _mem_kv_kernel(mem_ref, g_ref, w_ref, gk_ref, knt_ref, v_ref):
    h = (_rms(mem_ref[0]) * g_ref[0]).astype(BF16)
    kv = _dot(h, w_ref[0])
    gk = gk_ref[0]
    for hd in range(MEM_HEADS):
        lo = hd * MEM_HEAD_DIM
        kn = _rms(kv[:, lo:lo + MEM_HEAD_DIM]) * gk
        knt_ref[0, 0, lo:lo + MEM_HEAD_DIM, :] = kn.T.astype(BF16)
    v_ref[0, 0] = kv[:, MEM_WIDTH:].astype(BF16)


def _mem_kv(mem, g_mem, w_mem_kv, g_k):
    b = mem.shape[0]
    return pl.pallas_call(
        _mem_kv_kernel,
        grid=(DEPTH, b),
        in_specs=[
            pl.BlockSpec((1, N_MEM, D_MODEL), lambda d, i: (i, 0, 0)),
            pl.BlockSpec((1, 1, D_MODEL), lambda d, i: (d, 0, 0)),
            pl.BlockSpec((1, D_MODEL, 2 * MEM_WIDTH), lambda d, i: (d, 0, 0)),
            pl.BlockSpec((1, 1, MEM_HEAD_DIM), lambda d, i: (d, 0, 0)),
        ],
        out_specs=[
            pl.BlockSpec((1, 1, MEM_WIDTH, N_MEM), lambda d, i: (d, i, 0, 0)),
            pl.BlockSpec((1, 1, N_MEM, MEM_WIDTH), lambda d, i: (d, i, 0, 0)),
        ],
        out_shape=[
            jax.ShapeDtypeStruct((DEPTH, b, MEM_WIDTH, N_MEM), BF16),
            jax.ShapeDtypeStruct((DEPTH, b, N_MEM, MEM_WIDTH), BF16),
        ],
        compiler_params=pltpu.CompilerParams(
            dimension_semantics=("arbitrary", "arbitrary"), vmem_limit_bytes=VMEM_LIMIT),
        name="mem_kv",
    )(mem, g_mem.reshape(DEPTH, 1, D_MODEL), w_mem_kv, g_k.reshape(DEPTH, 1, MEM_HEAD_DIM))


def _mem_attention(h, w_ref, gq_ref, knt_ref, v_ref, out_ref, rows):
    qm = _dot(h, w_ref[:, 3 * MIX_WIDTH:])
    gq = gq_ref[...]
    for hd in range(MEM_HEADS):
        lo = hd * MEM_HEAD_DIM
        qn = (_rms(qm[:, lo:lo + MEM_HEAD_DIM]) * gq).astype(BF16)
        s = _dot(qn, knt_ref[0, lo:lo + MEM_HEAD_DIM, :]) * (MEM_HEAD_DIM ** -0.5)
        e = jnp.exp(s - jnp.max(s, axis=-1, keepdims=True))
        inv = 1.0 / jnp.sum(e, axis=-1, keepdims=True)
        o = _dot(e.astype(BF16), v_ref[0, :, lo:lo + MEM_HEAD_DIM]) * inv
        out_ref[0, rows, lo:lo + MEM_HEAD_DIM] = o.astype(BF16)


def _in_conv_kernel(x_ref, g_ref, w_ref, wc_ref, gq_ref, knt_ref, v_ref, mix_ref, mo_ref, ybuf):
    @pl.when(pl.program_id(1) == 0)
    def _():
        ybuf[0:SUBLANES, :] = jnp.zeros((SUBLANES, MIX_WIDTH), F32)

    for sub in range(SUBTILES):
        rows = pl.ds(sub * TM, TM)
        h = (_rms(x_ref[0, rows, :]) * g_ref[...]).astype(BF16)
        y = _dot(h, w_ref[:, MIX_WIDTH:2 * MIX_WIDTH]) * _dot(h, w_ref[:, 2 * MIX_WIDTH:3 * MIX_WIDTH])
        ybuf[SUBLANES:SUBLANES + TM, :] = y
        conv = (wc_ref[2:3, :] * y
                + wc_ref[1:2, :] * ybuf[SUBLANES - 1:SUBLANES - 1 + TM, :]
                + wc_ref[0:1, :] * ybuf[SUBLANES - 2:SUBLANES - 2 + TM, :])
        ybuf[0:SUBLANES, :] = ybuf[TM:TM + SUBLANES, :]
        mix_ref[0, rows, :] = (_dot(h, w_ref[:, 0:MIX_WIDTH]) * conv).astype(BF16)
        _mem_attention(h, w_ref, gq_ref, knt_ref, v_ref, mo_ref, rows)


def _in_sb_kernel(x_ref, g_ref, w_ref, gq_ref, knt_ref, v_ref, qkv_ref, mo_ref):
    for sub in range(SUBTILES):
        rows = pl.ds(sub * TM, TM)
        h = (_rms(x_ref[0, rows, :]) * g_ref[...]).astype(BF16)
        qkv_ref[0, rows, 0:MIX_WIDTH] = (
            _dot(h, w_ref[:, 0:MIX_WIDTH]) * (LOG2E * SB_HEAD_DIM ** -0.5)).astype(BF16)
        qkv_ref[0, rows, MIX_WIDTH:3 * MIX_WIDTH] = (
            _dot(h, w_ref[:, MIX_WIDTH:3 * MIX_WIDTH]).astype(BF16))
        _mem_attention(h, w_ref, gq_ref, knt_ref, v_ref, mo_ref, rows)


def _in_proj(layer, x, g_mix, w_in, w_conv, g_q, knt, v):
    b, s, _ = x.shape
    conv_layer = layer % N_MIXERS == 0
    tok = lambda i, j: (i, j, 0)
    in_specs = [
        pl.BlockSpec((1, TILE, D_MODEL), tok),
        _layer_param(layer, 1, D_MODEL),
        _layer_param(layer, D_MODEL, IN_WIDTH),
    ]
    args = [x, g_mix.reshape(DEPTH, 1, D_MODEL), w_in]
    if conv_layer:
        in_specs.append(_layer_param(layer // N_MIXERS, CONV_WIDTH, MIX_WIDTH))
        args.append(w_conv)
    in_specs += [
        _layer_param(layer, 1, MEM_HEAD_DIM),
        pl.BlockSpec((None, 1, MEM_WIDTH, N_MEM), lambda i, j: (layer, i, 0, 0)),
        pl.BlockSpec((None, 1, N_MEM, MEM_WIDTH), lambda i, j: (layer, i, 0, 0)),
    ]
    args += [g_q.reshape(DEPTH, 1, MEM_HEAD_DIM), knt, v]
    mix_width = MIX_WIDTH if conv_layer else 3 * MIX_WIDTH
    return pl.pallas_call(
        _in_conv_kernel if conv_layer else _in_sb_kernel,
        grid=(b, s // TILE),
        in_specs=in_specs,
        out_specs=[
            pl.BlockSpec((1, TILE, mix_width), tok),
            pl.BlockSpec((1, TILE, MEM_WIDTH), tok),
        ],
        out_shape=[
            jax.ShapeDtypeStruct((b, s, mix_width), BF16),
            jax.ShapeDtypeStruct((b, s, MEM_WIDTH), BF16),
        ],
        scratch_shapes=[pltpu.VMEM((SUBLANES + TM, MIX_WIDTH), F32)] if conv_layer else [],
        compiler_params=pltpu.CompilerParams(
            dimension_semantics=("arbitrary", "arbitrary"), vmem_limit_bytes=VMEM_LIMIT),
        name="in_conv" if conv_layer else "in_sb",
    )(*args)


def _sb_kernel(q_ref, k_ref, v_ref, o_ref, acc_ref, c_ref):
    step = pl.program_id(2)
    lane = lax.broadcasted_iota(jnp.int32, (1, LANES), 1)
    head_masks = [(lane >= hh * SB_HEAD_DIM) & (lane < (hh + 1) * SB_HEAD_DIM)
                  for hh in range(HEADS_PER_COL)]
    n_heads = SB_COLS * HEADS_PER_COL
    tri = (lax.broadcasted_iota(jnp.int32, (TK, TK), 0)
           >= lax.broadcasted_iota(jnp.int32, (TK, TK), 1)).astype(BF16)
    row = lax.broadcasted_iota(jnp.int32, (TQ, TK), 0)
    col = lax.broadcasted_iota(jnp.int32, (TQ, TK), 1)
    n_diag = TQ // TK

    def split_heads(sub):
        q_all = q_ref[0, sub * TQ:(sub + 1) * TQ, :]
        q_heads = []
        for cg in range(SB_COLS):
            qc = q_all[:, cg * LANES:(cg + 1) * LANES]
            q_heads += [jnp.where(m, qc, jnp.zeros_like(qc)) for m in head_masks]
        return q_heads

    def sweep_block(q_heads, start, mask, carries):
        kb = k_ref[0, pl.ds(start, TK), :]
        vb = v_ref[0, pl.ds(start, TK), :]
        k_cols = [kb[:, cg * LANES:(cg + 1) * LANES] for cg in range(SB_COLS)]
        zs = [_dot_nt(q_heads[h], k_cols[h // HEADS_PER_COL]) for h in range(n_heads)]
        suffixes = []
        for z in zs:
            sp = jnp.maximum(z, jnp.log2(1.0 + jnp.exp2(jnp.minimum(z, 126.0))))
            if mask is not None:
                sp = jnp.where(mask, sp, 0.0)
            suffixes.append(_dot(sp.astype(BF16), tri))
        weights = []
        for h in range(n_heads):
            a = jnp.exp2(zs[h] - suffixes[h] - carries[h])
            if mask is not None:
                a = jnp.where(mask, a, 0.0)
            weights.append(a.astype(BF16))
        carries = [carries[h] + suffixes[h][:, 0:1] for h in range(n_heads)]
        outs = []
        for cg in range(SB_COLS):
            v2 = vb[:, cg * LANES:(cg + 1) * LANES]
            v_heads = jnp.concatenate([jnp.where(m, v2, jnp.zeros_like(v2)) for m in head_masks], axis=0)
            w_cat = jnp.concatenate(weights[cg * HEADS_PER_COL:(cg + 1) * HEADS_PER_COL], axis=1)
            outs.append(_dot(w_cat, v_heads))
        return outs, carries

    def lowest(carries):
        low = carries[0]
        for c in carries[1:]:
            low = jnp.minimum(low, c)
        return jnp.min(low)

    def head_block(sub, qi, with_previous):
        q_heads = split_heads(sub)
        outs = [jnp.zeros((TQ, LANES), F32)] * SB_COLS
        carries = [jnp.zeros((TQ, 1), F32)] * n_heads
        for d in range(n_diag):
            off = (n_diag - 1 - d) * TK
            part, carries = sweep_block(q_heads, pl.multiple_of(qi * TQ + off, TK), col + off < row,
                                        carries)
            outs = [o + p for o, p in zip(outs, part)]
        if with_previous:
            part, carries = sweep_block(q_heads, pl.multiple_of((qi * n_diag - 1) * TK, TK), None,
                                        carries)
            outs = [o + p for o, p in zip(outs, part)]
        acc_ref[sub] = jnp.concatenate(outs, axis=1)
        for h in range(n_heads):
            c_ref[sub * n_heads + h] = carries[h]
        return lowest(carries)

    def later_step():
        return tuple(head_block(sub, step * SB_SUBS + sub, True) for sub in range(SB_SUBS))

    def first_step():
        return tuple(head_block(sub, sub, sub > 0) for sub in range(SB_SUBS))

    floors = lax.cond(step > 0, later_step, first_step)

    for sub in range(SB_SUBS):
        n_prev = (step * SB_SUBS + sub) * n_diag
        q_heads = split_heads(sub)

        def more(carry):
            i, c_low = carry
            return (i < n_prev) & (c_low < SKIP_LOG2)

        def body(carry):
            i, _ = carry
            part, carries = sweep_block(q_heads, pl.multiple_of((n_prev - 1 - i) * TK, TK), None,
                                        [c_ref[sub * n_heads + h] for h in range(n_heads)])
            for cg in range(SB_COLS):
                acc_ref[sub, :, cg * LANES:(cg + 1) * LANES] += part[cg]
            for h in range(n_heads):
                c_ref[sub * n_heads + h] = carries[h]
            return i + 1, lowest(carries)

        lax.while_loop(more, body, (jnp.int32(1), floors[sub]))
        o_ref[0, sub * TQ:(sub + 1) * TQ, :] = acc_ref[sub].astype(BF16)


def _stick_breaking(qkv):
    b, s, _ = qkv.shape
    width = SB_COLS * LANES
    n_col = MIX_WIDTH // width
    return pl.pallas_call(
        _sb_kernel,
        grid=(b, n_col, s // (SB_SUBS * TQ)),
        in_specs=[
            pl.BlockSpec((1, SB_SUBS * TQ, width), lambda i, p, j: (i, j, p)),
            pl.BlockSpec((1, s, width), lambda i, p, j: (i, 0, n_col + p)),
            pl.BlockSpec((1, s, width), lambda i, p, j: (i, 0, 2 * n_col + p)),
        ],
        out_specs=pl.BlockSpec((1, SB_SUBS * TQ, width), lambda i, p, j: (i, j, p)),
        out_shape=jax.ShapeDtypeStruct((b, s, MIX_WIDTH), BF16),
        scratch_shapes=[
            pltpu.VMEM((SB_SUBS, TQ, width), F32),
            pltpu.VMEM((SB_SUBS * SB_COLS * HEADS_PER_COL, TQ, 1), F32),
        ],
        compiler_params=pltpu.CompilerParams(
            dimension_semantics=("arbitrary", "arbitrary", "arbitrary"),
            vmem_limit_bytes=VMEM_LIMIT),
        name="stick_breaking",
    )(qkv, qkv, qkv)


def _out_ffn_kernel(x_ref, mix_ref, mo_ref, wo_ref, g_ref, wi_ref, wf_ref, out_ref):
    for sub in range(SUBTILES):
        rows = pl.ds(sub * TM, TM)
        x1 = (x_ref[0, rows, :] + _dot(mix_ref[0, rows, :], wo_ref[0:MIX_WIDTH, :])
              + _dot(mo_ref[0, rows, :], wo_ref[MIX_WIDTH:OUT_WIDTH, :]))
        h = (_rms(x1) * g_ref[...]).astype(BF16)
        ffn = None
        for lo, hi in FFN_SPLITS:
            gate = _dot(h, wi_ref[:, lo:hi])
            up = _dot(h, wi_ref[:, D_FF + lo:D_FF + hi])
            act = (gate * jax.nn.sigmoid(gate) * up).astype(BF16)
            part = _dot(act, wf_ref[lo:hi, :])
            ffn = part if ffn is None else ffn + part
        out_ref[0, rows, :] = ffn + x1


def _out_ffn(layer, x, mixed, mem_out, w_out, g_ffn, w_ffn_in, w_ffn_out):
    b, s, _ = x.shape
    tok = lambda i, j: (i, j, 0)
    return pl.pallas_call(
        _out_ffn_kernel,
        grid=(b, s // TILE),
        in_specs=[
            pl.BlockSpec((1, TILE, D_MODEL), tok),
            pl.BlockSpec((1, TILE, MIX_WIDTH), tok),
            pl.BlockSpec((1, TILE, MEM_WIDTH), tok),
            _layer_param(layer, OUT_WIDTH, D_MODEL),
            _layer_param(layer, 1, D_MODEL),
            _layer_param(layer, D_MODEL, 2 * D_FF),
            _layer_param(layer, D_FF, D_MODEL),
        ],
        out_specs=pl.BlockSpec((1, TILE, D_MODEL), tok),
        out_shape=jax.ShapeDtypeStruct((b, s, D_MODEL), F32),
        compiler_params=pltpu.CompilerParams(
            dimension_semantics=("arbitrary", "arbitrary"), vmem_limit_bytes=VMEM_LIMIT),
        name="out_ffn",
    )(x, mixed, mem_out, w_out, g_ffn.reshape(DEPTH, 1, D_MODEL), w_ffn_in, w_ffn_out)


def kernel(x, mem, w_in, w_out, w_mem_kv, w_conv, w_ffn_in, w_ffn_out, g_mix, g_mem, g_ffn, g_q, g_k):
    assert x.shape[1] % TILE == 0 and x.shape[1] % (SB_SUBS * TQ) == 0 and TQ % TK == 0
    assert MIX_WIDTH % (SB_COLS * LANES) == 0
    w_in, w_out, w_mem_kv, w_ffn_in, w_ffn_out = (
        w.astype(BF16) for w in (w_in, w_out, w_mem_kv, w_ffn_in, w_ffn_out))
    knt, v = _mem_kv(mem, g_mem, w_mem_kv, g_k)
    for layer in range(DEPTH):
        mixed, mem_out = _in_proj(layer, x, g_mix, w_in, w_conv, g_q, knt, v)
        if layer % N_MIXERS == 1:
            mixed = _stick_breaking(mixed)
        x = _out_ffn(layer, x, mixed, mem_out, w_out, g_ffn, w_ffn_in, w_ffn_out)
    return x
```

```python
import math

import jax
import jax.numpy as jnp
from jax import lax
from jax.experimental import pallas as pl
from jax.experimental.pallas import tpu as pltpu

D_MODEL = 1024
DEPTH = 4
N_MIXERS = 2
CONV_WIDTH = 3
MIX_WIDTH = D_MODEL
SB_HEADS = 16
SB_HEAD_DIM = MIX_WIDTH // SB_HEADS
N_MEM = 256
MEM_HEADS = 4
MEM_HEAD_DIM = 128
MEM_WIDTH = MEM_HEADS * MEM_HEAD_DIM
IN_WIDTH = 3 * MIX_WIDTH + MEM_WIDTH
OUT_WIDTH = MIX_WIDTH + MEM_WIDTH
D_FF = -(-(8 * D_MODEL) // (3 * 256)) * 256
EPS = 1e-6

LANES = 128
SUBLANES = 8
VMEM_LIMIT = 56 * 1024 * 1024

TM = 512
SUBTILES = 2
TILE = SUBTILES * TM
TQ = 256
TK = 256
MXU_WIDTH = 256
FFN_SPLITS = ((0, 6 * MXU_WIDTH), (6 * MXU_WIDTH, D_FF))
HEADS_PER_COL = LANES // SB_HEAD_DIM
SB_COLS = 2
SB_SUBS = 4
LOG2E = math.log2(math.e)
SKIP_LOG2 = 160.0

BF16 = jnp.bfloat16
F32 = jnp.float32


def _dot(a, b):
    return jnp.dot(a, b, preferred_element_type=F32)


def _dot_nt(a, b):
    return lax.dot_general(a, b, (((1,), (1,)), ((), ())), preferred_element_type=F32)


def _rms(x):
    return x * lax.rsqrt(jnp.mean(x * x, axis=-1, keepdims=True) + EPS)


def _layer_param(layer, *block_shape):
    zeros = (0,) * len(block_shape)
    return pl.BlockSpec((None,) + block_shape, lambda i, j: (layer,) + zeros,
                        pipeline_mode=pl.Buffered(1))


def _mem_kv_kernel(mem_ref, g_ref, w_ref, gk_ref, knt_ref, v_ref):
    gk = gk_ref[0]
    for i in range(mem_ref.shape[0]):
        h = (_rms(mem_ref[i]) * g_ref[0]).astype(BF16)
        kv = _dot(h, w_ref[0])
        for hd in range(MEM_HEADS):
            lo = hd * MEM_HEAD_DIM
            kn = _rms(kv[:, lo:lo + MEM_HEAD_DIM]) * gk
            knt_ref[0, i, lo:lo + MEM_HEAD_DIM, :] = kn.T.astype(BF16)
        v_ref[0, i] = kv[:, MEM_WIDTH:].astype(BF16)


def _mem_kv(mem, g_mem, w_mem_kv, g_k):
    b = mem.shape[0]
    return pl.pallas_call(
        _mem_kv_kernel,
        grid=(DEPTH,),
        in_specs=[
            pl.BlockSpec((b, N_MEM, D_MODEL), lambda d: (0, 0, 0)),
            pl.BlockSpec((1, 1, D_MODEL), lambda d: (d, 0, 0)),
            pl.BlockSpec((1, D_MODEL, 2 * MEM_WIDTH), lambda d: (d, 0, 0)),
            pl.BlockSpec((1, 1, MEM_HEAD_DIM), lambda d: (d, 0, 0)),
        ],
        out_specs=[
            pl.BlockSpec((1, b, MEM_WIDTH, N_MEM), lambda d: (d, 0, 0, 0)),
            pl.BlockSpec((1, b, N_MEM, MEM_WIDTH), lambda d: (d, 0, 0, 0)),
        ],
        out_shape=[
            jax.ShapeDtypeStruct((DEPTH, b, MEM_WIDTH, N_MEM), BF16),
            jax.ShapeDtypeStruct((DEPTH, b, N_MEM, MEM_WIDTH), BF16),
        ],
        compiler_params=pltpu.CompilerParams(
            dimension_semantics=("arbitrary",), vmem_limit_bytes=VMEM_LIMIT),
        name="mem_kv",
    )(mem, g_mem.reshape(DEPTH, 1, D_MODEL), w_mem_kv, g_k.reshape(DEPTH, 1, MEM_HEAD_DIM))


def _mem_attention(h, w_ref, gq_ref, knt_ref, v_ref, out_ref, rows):
    qm = _dot(h, w_ref[:, 3 * MIX_WIDTH:])
    gq = gq_ref[...]
    for hd in range(MEM_HEADS):
        lo = hd * MEM_HEAD_DIM
        qn = (_rms(qm[:, lo:lo + MEM_HEAD_DIM]) * gq).astype(BF16)
        s = _dot(qn, knt_ref[0, lo:lo + MEM_HEAD_DIM, :]) * (MEM_HEAD_DIM ** -0.5)
        e = jnp.exp(s - jnp.max(s, axis=-1, keepdims=True))
        inv = 1.0 / jnp.sum(e, axis=-1, keepdims=True)
        o = _dot(e.astype(BF16), v_ref[0, :, lo:lo + MEM_HEAD_DIM]) * inv
        out_ref[0, rows, lo:lo + MEM_HEAD_DIM] = o.astype(BF16)


def _in_conv_kernel(x_ref, g_ref, w_ref, wc_ref, gq_ref, knt_ref, v_ref, mix_ref, mo_ref, ybuf):
    @pl.when(pl.program_id(1) == 0)
    def _():
        ybuf[0:SUBLANES, :] = jnp.zeros((SUBLANES, MIX_WIDTH), F32)

    for sub in range(SUBTILES):
        rows = pl.ds(sub * TM, TM)
        h = (_rms(x_ref[0, rows, :]) * g_ref[...]).astype(BF16)
        y = _dot(h, w_ref[:, MIX_WIDTH:2 * MIX_WIDTH]) * _dot(h, w_ref[:, 2 * MIX_WIDTH:3 * MIX_WIDTH])
        ybuf[SUBLANES:SUBLANES + TM, :] = y
        conv = (wc_ref[2:3, :] * y
                + wc_ref[1:2, :] * ybuf[SUBLANES - 1:SUBLANES - 1 + TM, :]
                + wc_ref[0:1, :] * ybuf[SUBLANES - 2:SUBLANES - 2 + TM, :])
        ybuf[0:SUBLANES, :] = ybuf[TM:TM + SUBLANES, :]
        mix_ref[0, rows, :] = (_dot(h, w_ref[:, 0:MIX_WIDTH]) * conv).astype(BF16)
        _mem_attention(h, w_ref, gq_ref, knt_ref, v_ref, mo_ref, rows)


def _in_sb_kernel(x_ref, g_ref, w_ref, gq_ref, knt_ref, v_ref, qkv_ref, mo_ref):
    for sub in range(SUBTILES):
        rows = pl.ds(sub * TM, TM)
        h = (_rms(x_ref[0, rows, :]) * g_ref[...]).astype(BF16)
        qkv_ref[0, rows, 0:MIX_WIDTH] = (
            _dot(h, w_ref[:, 0:MIX_WIDTH]) * (LOG2E * SB_HEAD_DIM ** -0.5)).astype(BF16)
        qkv_ref[0, rows, MIX_WIDTH:3 * MIX_WIDTH] = (
            _dot(h, w_ref[:, MIX_WIDTH:3 * MIX_WIDTH]).astype(BF16))
        _mem_attention(h, w_ref, gq_ref, knt_ref, v_ref, mo_ref, rows)


def _in_proj(layer, x, g_mix, w_in, w_conv, g_q, knt, v):
    b, s, _ = x.shape
    conv_layer = layer % N_MIXERS == 0
    tok = lambda i, j: (i, j, 0)
    in_specs = [
        pl.BlockSpec((1, TILE, D_MODEL), tok),
        _layer_param(layer, 1, D_MODEL),
        _layer_param(layer, D_MODEL, IN_WIDTH),
    ]
    args = [x, g_mix.reshape(DEPTH, 1, D_MODEL), w_in]
    if conv_layer:
        in_specs.append(_layer_param(layer // N_MIXERS, CONV_WIDTH, MIX_WIDTH))
        args.append(w_conv)
    in_specs += [
        _layer_param(layer, 1, MEM_HEAD_DIM),
        pl.BlockSpec((None, 1, MEM_WIDTH, N_MEM), lambda i, j: (layer, i, 0, 0)),
        pl.BlockSpec((None, 1, N_MEM, MEM_WIDTH), lambda i, j: (layer, i, 0, 0)),
    ]
    args += [g_q.reshape(DEPTH, 1, MEM_HEAD_DIM), knt, v]
    mix_width = MIX_WIDTH if conv_layer else 3 * MIX_WIDTH
    return pl.pallas_call(
        _in_conv_kernel if conv_layer else _in_sb_kernel,
        grid=(b, s // TILE),
        in_specs=in_specs,
        out_specs=[
            pl.BlockSpec((1, TILE, mix_width), tok),
            pl.BlockSpec((1, TILE, MEM_WIDTH), tok),
        ],
        out_shape=[
            jax.ShapeDtypeStruct((b, s, mix_width), BF16),
            jax.ShapeDtypeStruct((b, s, MEM_WIDTH), BF16),
        ],
        scratch_shapes=[pltpu.VMEM((SUBLANES + TM, MIX_WIDTH), F32)] if conv_layer else [],
        compiler_params=pltpu.CompilerParams(
            dimension_semantics=("arbitrary", "arbitrary"), vmem_limit_bytes=VMEM_LIMIT),
        name="in_conv" if conv_layer else "in_sb",
    )(*args)


def _sb_kernel(q_ref, k_ref, v_ref, o_ref, acc_ref, c_ref):
    step = pl.program_id(2)
    lane = lax.broadcasted_iota(jnp.int32, (1, LANES), 1)
    head_masks = [(lane >= hh * SB_HEAD_DIM) & (lane < (hh + 1) * SB_HEAD_DIM)
                  for hh in range(HEADS_PER_COL)]
    n_heads = SB_COLS * HEADS_PER_COL
    tri = (lax.broadcasted_iota(jnp.int32, (TK, TK), 0)
           >= lax.broadcasted_iota(jnp.int32, (TK, TK), 1)).astype(BF16)
    row = lax.broadcasted_iota(jnp.int32, (TQ, TK), 0)
    col = lax.broadcasted_iota(jnp.int32, (TQ, TK), 1)
    n_diag = TQ // TK

    def split_heads(sub):
        q_all = q_ref[0, sub * TQ:(sub + 1) * TQ, :]
        q_heads = []
        for cg in range(SB_COLS):
            qc = q_all[:, cg * LANES:(cg + 1) * LANES]
            q_heads += [jnp.where(m, qc, jnp.zeros_like(qc)) for m in head_masks]
        return q_heads

    def sweep_block(q_heads, start, mask, carries):
        kb = k_ref[0, pl.ds(start, TK), :]
        vb = v_ref[0, pl.ds(start, TK), :]
        k_cols = [kb[:, cg * LANES:(cg + 1) * LANES] for cg in range(SB_COLS)]
        zs = [_dot_nt(q_heads[h], k_cols[h // HEADS_PER_COL]) for h in range(n_heads)]
        suffixes = []
        for z in zs:
            sp = jnp.maximum(z, jnp.log2(1.0 + jnp.exp2(jnp.minimum(z, 126.0))))
            if mask is not None:
                sp = jnp.where(mask, sp, 0.0)
            suffixes.append(_dot(sp.astype(BF16), tri))
        weights = []
        for h in range(n_heads):
            a = jnp.exp2(zs[h] - suffixes[h] - carries[h])
            if mask is not None:
                a = jnp.where(mask, a, 0.0)
            weights.append(a.astype(BF16))
        carries = [carries[h] + suffixes[h][:, 0:1] for h in range(n_heads)]
        outs = []
        for cg in range(SB_COLS):
            v2 = vb[:, cg * LANES:(cg + 1) * LANES]
            v_heads = jnp.concatenate([jnp.where(m, v2, jnp.zeros_like(v2)) for m in head_masks], axis=0)
            w_cat = jnp.concatenate(weights[cg * HEADS_PER_COL:(cg + 1) * HEADS_PER_COL], axis=1)
            outs.append(_dot(w_cat, v_heads))
        return outs, carries

    def lowest(carries):
        low = carries[0]
        for c in carries[1:]:
            low = jnp.minimum(low, c)
        return jnp.min(low)

    def head_block(sub, qi, with_previous):
        q_heads = split_heads(sub)
        outs = [jnp.zeros((TQ, LANES), F32)] * SB_COLS
        carries = [jnp.zeros((TQ, 1), F32)] * n_heads
        for d in range(n_diag):
            off = (n_diag - 1 - d) * TK
            part, carries = sweep_block(q_heads, pl.multiple_of(qi * TQ + off, TK), col + off < row,
                                        carries)
            outs = [o + p for o, p in zip(outs, part)]
        if with_previous:
            part, carries = sweep_block(q_heads, pl.multiple_of((qi * n_diag - 1) * TK, TK), None,
                                        carries)
            outs = [o + p for o, p in zip(outs, part)]
        acc_ref[sub] = jnp.concatenate(outs, axis=1)
        for h in range(n_heads):
            c_ref[sub * n_heads + h] = carries[h]
        return lowest(carries)

    def later_step():
        return tuple(head_block(sub, step * SB_SUBS + sub, True) for sub in range(SB_SUBS))

    def first_step():
        return tuple(head_block(sub, sub, sub > 0) for sub in range(SB_SUBS))

    floors = lax.cond(step > 0, later_step, first_step)

    for sub in range(SB_SUBS):
        n_prev = (step * SB_SUBS + sub) * n_diag
        q_heads = split_heads(sub)

        def more(carry):
            i, c_low = carry
            return (i < n_prev) & (c_low < SKIP_LOG2)

        def body(carry):
            i, _ = carry
            part, carries = sweep_block(q_heads, pl.multiple_of((n_prev - 1 - i) * TK, TK), None,
                                        [c_ref[sub * n_heads + h] for h in range(n_heads)])
            for cg in range(SB_COLS):
                acc_ref[sub, :, cg * LANES:(cg + 1) * LANES] += part[cg]
            for h in range(n_heads):
                c_ref[sub * n_heads + h] = carries[h]
            return i + 1, lowest(carries)

        lax.while_loop(more, body, (jnp.int32(1), floors[sub]))
        o_ref[0, sub * TQ:(sub + 1) * TQ, :] = acc_ref[sub].astype(BF16)


def _stick_breaking(qkv):
    b, s, _ = qkv.shape
    width = SB_COLS * LANES
    n_col = MIX_WIDTH // width
    return pl.pallas_call(
        _sb_kernel,
        grid=(b, n_col, s // (SB_SUBS * TQ)),
        in_specs=[
            pl.BlockSpec((1, SB_SUBS * TQ, width), lambda i, p, j: (i, j, p)),
            pl.BlockSpec((1, s, width), lambda i, p, j: (i, 0, n_col + p)),
            pl.BlockSpec((1, s, width), lambda i, p, j: (i, 0, 2 * n_col + p)),
        ],
        out_specs=pl.BlockSpec((1, SB_SUBS * TQ, width), lambda i, p, j: (i, j, p)),
        out_shape=jax.ShapeDtypeStruct((b, s, MIX_WIDTH), BF16),
        scratch_shapes=[
            pltpu.VMEM((SB_SUBS, TQ, width), F32),
            pltpu.VMEM((SB_SUBS * SB_COLS * HEADS_PER_COL, TQ, 1), F32),
        ],
        compiler_params=pltpu.CompilerParams(
            dimension_semantics=("arbitrary", "arbitrary", "arbitrary"),
            vmem_limit_bytes=VMEM_LIMIT),
        name="stick_breaking",
    )(qkv, qkv, qkv)


def _out_ffn_kernel(x_ref, mix_ref, mo_ref, wo_ref, g_ref, wi_ref, wf_ref, out_ref):
    for sub in range(SUBTILES):
        rows = pl.ds(sub * TM, TM)
        x1 = (x_ref[0, rows, :] + _dot(mix_ref[0, rows, :], wo_ref[0:MIX_WIDTH, :])
              + _dot(mo_ref[0, rows, :], wo_ref[MIX_WIDTH:OUT_WIDTH, :]))
        h = (_rms(x1) * g_ref[...]).astype(BF16)
        ffn = None
        for lo, hi in FFN_SPLITS:
            gate = _dot(h, wi_ref[:, lo:hi])
            up = _dot(h, wi_ref[:, D_FF + lo:D_FF + hi])
            act = (gate * jax.nn.sigmoid(gate) * up).astype(BF16)
            part = _dot(act, wf_ref[lo:hi, :])
            ffn = part if ffn is None else ffn + part
        out_ref[0, rows, :] = ffn + x1


def _out_ffn(layer, x, mixed, mem_out, w_out, g_ffn, w_ffn_in, w_ffn_out):
    b, s, _ = x.shape
    tok = lambda i, j: (i, j, 0)
    return pl.pallas_call(
        _out_ffn_kernel,
        grid=(b, s // TILE),
        in_specs=[
            pl.BlockSpec((1, TILE, D_MODEL), tok),
            pl.BlockSpec((1, TILE, MIX_WIDTH), tok),
            pl.BlockSpec((1, TILE, MEM_WIDTH), tok),
            _layer_param(layer, OUT_WIDTH, D_MODEL),
            _layer_param(layer, 1, D_MODEL),
            _layer_param(layer, D_MODEL, 2 * D_FF),
            _layer_param(layer, D_FF, D_MODEL),
        ],
        out_specs=pl.BlockSpec((1, TILE, D_MODEL), tok),
        out_shape=jax.ShapeDtypeStruct((b, s, D_MODEL), F32),
        compiler_params=pltpu.CompilerParams(
            dimension_semantics=("arbitrary", "arbitrary"), vmem_limit_bytes=VMEM_LIMIT),
        name="out_ffn",
    )(x, mixed, mem_out, w_out, g_ffn.reshape(DEPTH, 1, D_MODEL), w_ffn_in, w_ffn_out)


def kernel(x, mem, w_in, w_out, w_mem_kv, w_conv, w_ffn_in, w_ffn_out, g_mix, g_mem, g_ffn, g_q, g_k):
    assert x.shape[1] % TILE == 0 and x.shape[1] % (SB_SUBS * TQ) == 0 and TQ % TK == 0
    assert MIX_WIDTH % (SB_COLS * LANES) == 0
    w_in, w_out, w_mem_kv, w_ffn_in, w_ffn_out = (
        w.astype(BF16) for w in (w_in, w_out, w_mem_kv, w_ffn_in, w_ffn_out))
    knt, v = _mem_kv(mem, g_mem, w_mem_kv, g_k)
    for layer in range(DEPTH):
        mixed, mem_out = _in_proj(layer, x, g_mix, w_in, w_conv, g_q, knt, v)
        if layer % N_MIXERS == 1:
            mixed = _stick_breaking(mixed)
        x = _out_ffn(layer, x, mixed, mem_out, w_out, g_ffn, w_ffn_in, w_ffn_out)
    return x
```

```python
import math

import jax
import jax.numpy as jnp
from jax import lax
from jax.experimental import pallas as pl
from jax.experimental.pallas import tpu as pltpu

D_MODEL = 1024
DEPTH = 4
N_MIXERS = 2
CONV_WIDTH = 3
MIX_WIDTH = D_MODEL
SB_HEADS = 16
SB_HEAD_DIM = MIX_WIDTH // SB_HEADS
N_MEM = 256
MEM_HEADS = 4
MEM_HEAD_DIM = 128
MEM_WIDTH = MEM_HEADS * MEM_HEAD_DIM
IN_WIDTH = 3 * MIX_WIDTH + MEM_WIDTH
OUT_WIDTH = MIX_WIDTH + MEM_WIDTH
D_FF = -(-(8 * D_MODEL) // (3 * 256)) * 256
EPS = 1e-6

LANES = 128
SUBLANES = 8
VMEM_LIMIT = 56 * 1024 * 1024

TM = 512
SUBTILES = 2
TILE = SUBTILES * TM
TQ = 256
TK = 256
MXU_WIDTH = 256
FFN_SPLITS = ((0, 6 * MXU_WIDTH), (6 * MXU_WIDTH, D_FF))
HEADS_PER_COL = LANES // SB_HEAD_DIM
SB_COLS = 2
SB_SUBS = 4
LOG2E = math.log2(math.e)
SKIP_LOG2 = 160.0

BF16 = jnp.bfloat16
F32 = jnp.float32


def _dot(a, b):
    return jnp.dot(a, b, preferred_element_type=F32)


def _dot_nt(a, b):
    return lax.dot_general(a, b, (((1,), (1,)), ((), ())), preferred_element_type=F32)


def _rms(x):
    return x * lax.rsqrt(jnp.mean(x * x, axis=-1, keepdims=True) + EPS)


def _layer_param(layer, *block_shape):
    zeros = (0,) * len(block_shape)
    return pl.BlockSpec((None,) + block_shape, lambda i, j: (layer,) + zeros,
                        pipeline_mode=pl.Buffered(1))


def _mem_kv_kernel(mem_ref, g_ref, w_ref, gk_ref, knt_ref, v_ref):
    gk = gk_ref[0]
    for i in range(mem_ref.shape[0]):
        h = (_rms(mem_ref[i]) * g_ref[0]).astype(BF16)
        kv = _dot(h, w_ref[0])
        for hd in range(MEM_HEADS):
            lo = hd * MEM_HEAD_DIM
            kn = _rms(kv[:, lo:lo + MEM_HEAD_DIM]) * gk
            knt_ref[0, i, lo:lo + MEM_HEAD_DIM, :] = kn.T.astype(BF16)
        v_ref[0, i] = kv[:, MEM_WIDTH:].astype(BF16)


def _mem_kv(mem, g_mem, w_mem_kv, g_k):
    b = mem.shape[0]
    return pl.pallas_call(
        _mem_kv_kernel,
        grid=(DEPTH,),
        in_specs=[
            pl.BlockSpec((b, N_MEM, D_MODEL), lambda d: (0, 0, 0)),
            pl.BlockSpec((1, 1, D_MODEL), lambda d: (d, 0, 0)),
            pl.BlockSpec((1, D_MODEL, 2 * MEM_WIDTH), lambda d: (d, 0, 0)),
            pl.BlockSpec((1, 1, MEM_HEAD_DIM), lambda d: (d, 0, 0)),
        ],
        out_specs=[
            pl.BlockSpec((1, b, MEM_WIDTH, N_MEM), lambda d: (d, 0, 0, 0)),
            pl.BlockSpec((1, b, N_MEM, MEM_WIDTH), lambda d: (d, 0, 0, 0)),
        ],
        out_shape=[
            jax.ShapeDtypeStruct((DEPTH, b, MEM_WIDTH, N_MEM), BF16),
            jax.ShapeDtypeStruct((DEPTH, b, N_MEM, MEM_WIDTH), BF16),
        ],
        compiler_params=pltpu.CompilerParams(
            dimension_semantics=("arbitrary",), vmem_limit_bytes=VMEM_LIMIT),
        name="mem_kv",
    )(mem, g_mem.reshape(DEPTH, 1, D_MODEL), w_mem_kv, g_k.reshape(DEPTH, 1, MEM_HEAD_DIM))


def _mem_attention(h, w_ref, gq_ref, knt_ref, v_ref, out_ref, rows):
    qm = _dot(h, w_ref[:, 3 * MIX_WIDTH:])
    gq = gq_ref[...]
    for hd in range(MEM_HEADS):
        lo = hd * MEM_HEAD_DIM
        qn = (_rms(qm[:, lo:lo + MEM_HEAD_DIM]) * gq).astype(BF16)
        s = _dot(qn, knt_ref[0, lo:lo + MEM_HEAD_DIM, :]) * (MEM_HEAD_DIM ** -0.5)
        e = jnp.exp(s - jnp.max(s, axis=-1, keepdims=True))
        inv = 1.0 / jnp.sum(e, axis=-1, keepdims=True)
        o = _dot(e.astype(BF16), v_ref[0, :, lo:lo + MEM_HEAD_DIM]) * inv
        out_ref[0, rows, lo:lo + MEM_HEAD_DIM] = o.astype(BF16)


def _in_conv_kernel(x_ref, g_ref, w_ref, wc_ref, gq_ref, knt_ref, v_ref, mix_ref, mo_ref, ybuf):
    @pl.when(pl.program_id(1) == 0)
    def _():
        ybuf[0:SUBLANES, :] = jnp.zeros((SUBLANES, MIX_WIDTH), F32)

    for sub in range(SUBTILES):
        rows = pl.ds(sub * TM, TM)
        h = (_rms(x_ref[0, rows, :]) * g_ref[...]).astype(BF16)
        _mem_attention(h, w_ref, gq_ref, knt_ref, v_ref, mo_ref, rows)
        y = _dot(h, w_ref[:, MIX_WIDTH:2 * MIX_WIDTH]) * _dot(h, w_ref[:, 2 * MIX_WIDTH:3 * MIX_WIDTH])
        ybuf[SUBLANES:SUBLANES + TM, :] = y
        conv = (wc_ref[2:3, :] * y
                + wc_ref[1:2, :] * ybuf[SUBLANES - 1:SUBLANES - 1 + TM, :]
                + wc_ref[0:1, :] * ybuf[SUBLANES - 2:SUBLANES - 2 + TM, :])
        ybuf[0:SUBLANES, :] = ybuf[TM:TM + SUBLANES, :]
        mix_ref[0, rows, :] = (_dot(h, w_ref[:, 0:MIX_WIDTH]) * conv).astype(BF16)


def _in_sb_kernel(x_ref, g_ref, w_ref, gq_ref, knt_ref, v_ref, qkv_ref, mo_ref):
    for sub in range(SUBTILES):
        rows = pl.ds(sub * TM, TM)
        h = (_rms(x_ref[0, rows, :]) * g_ref[...]).astype(BF16)
        _mem_attention(h, w_ref, gq_ref, knt_ref, v_ref, mo_ref, rows)
        qkv_ref[0, rows, 0:MIX_WIDTH] = (
            _dot(h, w_ref[:, 0:MIX_WIDTH]) * (LOG2E * SB_HEAD_DIM ** -0.5)).astype(BF16)
        qkv_ref[0, rows, MIX_WIDTH:3 * MIX_WIDTH] = (
            _dot(h, w_ref[:, MIX_WIDTH:3 * MIX_WIDTH]).astype(BF16))


def _in_proj(layer, x, g_mix, w_in, w_conv, g_q, knt, v):
    b, s, _ = x.shape
    conv_layer = layer % N_MIXERS == 0
    tok = lambda i, j: (i, j, 0)
    in_specs = [
        pl.BlockSpec((1, TILE, D_MODEL), tok),
        _layer_param(layer, 1, D_MODEL),
        _layer_param(layer, D_MODEL, IN_WIDTH),
    ]
    args = [x, g_mix.reshape(DEPTH, 1, D_MODEL), w_in]
    if conv_layer:
        in_specs.append(_layer_param(layer // N_MIXERS, CONV_WIDTH, MIX_WIDTH))
        args.append(w_conv)
    in_specs += [
        _layer_param(layer, 1, MEM_HEAD_DIM),
        pl.BlockSpec((None, 1, MEM_WIDTH, N_MEM), lambda i, j: (layer, i, 0, 0)),
        pl.BlockSpec((None, 1, N_MEM, MEM_WIDTH), lambda i, j: (layer, i, 0, 0)),
    ]
    args += [g_q.reshape(DEPTH, 1, MEM_HEAD_DIM), knt, v]
    mix_width = MIX_WIDTH if conv_layer else 3 * MIX_WIDTH
    return pl.pallas_call(
        _in_conv_kernel if conv_layer else _in_sb_kernel,
        grid=(b, s // TILE),
        in_specs=in_specs,
        out_specs=[
            pl.BlockSpec((1, TILE, mix_width), tok),
            pl.BlockSpec((1, TILE, MEM_WIDTH), tok),
        ],
        out_shape=[
            jax.ShapeDtypeStruct((b, s, mix_width), BF16),
            jax.ShapeDtypeStruct((b, s, MEM_WIDTH), BF16),
        ],
        scratch_shapes=[pltpu.VMEM((SUBLANES + TM, MIX_WIDTH), F32)] if conv_layer else [],
        compiler_params=pltpu.CompilerParams(
            dimension_semantics=("arbitrary", "arbitrary"), vmem_limit_bytes=VMEM_LIMIT),
        name="in_conv" if conv_layer else "in_sb",
    )(*args)


def _sb_kernel(q_ref, k_ref, v_ref, o_ref, acc_ref, c_ref):
    step = pl.program_id(2)
    lane = lax.broadcasted_iota(jnp.int32, (1, LANES), 1)
    head_masks = [(lane >= hh * SB_HEAD_DIM) & (lane < (hh + 1) * SB_HEAD_DIM)
                  for hh in range(HEADS_PER_COL)]
    n_heads = SB_COLS * HEADS_PER_COL
    tri = (lax.broadcasted_iota(jnp.int32, (TK, TK), 0)
           >= lax.broadcasted_iota(jnp.int32, (TK, TK), 1)).astype(BF16)
    row = lax.broadcasted_iota(jnp.int32, (TQ, TK), 0)
    col = lax.broadcasted_iota(jnp.int32, (TQ, TK), 1)
    n_diag = TQ // TK

    def split_heads(sub):
        q_all = q_ref[0, sub * TQ:(sub + 1) * TQ, :]
        q_heads = []
        for cg in range(SB_COLS):
            qc = q_all[:, cg * LANES:(cg + 1) * LANES]
            q_heads += [jnp.where(m, qc, jnp.zeros_like(qc)) for m in head_masks]
        return q_heads

    def sweep(q_heads, blocks, carries):
        kbs = [k_ref[0, pl.ds(start, TK), :] for start, _ in blocks]
        vbs = [v_ref[0, pl.ds(start, TK), :] for start, _ in blocks]
        weights = [[None] * n_heads for _ in blocks]
        carries = list(carries)
        for h in range(n_heads):
            cg = h // HEADS_PER_COL
            for bi, (_, mask) in enumerate(blocks):
                z = _dot_nt(q_heads[h], kbs[bi][:, cg * LANES:(cg + 1) * LANES])
                sp = jnp.maximum(z, jnp.log2(1.0 + jnp.exp2(jnp.minimum(z, 126.0))))
                if mask is not None:
                    sp = jnp.where(mask, sp, 0.0)
                suffix = _dot(sp.astype(BF16), tri)
                a = jnp.exp2(z - suffix - carries[h])
                if mask is not None:
                    a = jnp.where(mask, a, 0.0)
                weights[bi][h] = a.astype(BF16)
                carries[h] = carries[h] + suffix[:, 0:1]
        outs = []
        for cg in range(SB_COLS):
            heads = range(cg * HEADS_PER_COL, (cg + 1) * HEADS_PER_COL)
            w_cat = jnp.concatenate([weights[bi][h] for bi in range(len(blocks)) for h in heads], axis=1)
            v_cat = jnp.concatenate(
                [jnp.where(m, vbs[bi][:, cg * LANES:(cg + 1) * LANES], jnp.zeros((TK, LANES), BF16))
                 for bi in range(len(blocks)) for m in head_masks], axis=0)
            outs.append(_dot(w_cat, v_cat))
        return outs, carries

    def lowest(carries):
        low = carries[0]
        for c in carries[1:]:
            low = jnp.minimum(low, c)
        return jnp.min(low)

    def head_block(sub, qi, with_previous):
        blocks = []
        for d in range(n_diag):
            off = (n_diag - 1 - d) * TK
            blocks.append((pl.multiple_of(qi * TQ + off, TK), col + off < row))
        if with_previous:
            blocks.append((pl.multiple_of((qi * n_diag - 1) * TK, TK), None))
        outs, carries = sweep(split_heads(sub), blocks, [jnp.zeros((TQ, 1), F32)] * n_heads)
        acc_ref[sub] = jnp.concatenate(outs, axis=1)
        for h in range(n_heads):
            c_ref[sub * n_heads + h] = carries[h]
        return lowest(carries)

    def later_step():
        return tuple(head_block(sub, step * SB_SUBS + sub, True) for sub in range(SB_SUBS))

    def first_step():
        return tuple(head_block(sub, sub, sub > 0) for sub in range(SB_SUBS))

    floors = lax.cond(step > 0, later_step, first_step)

    for sub in range(SB_SUBS):
        n_prev = (step * SB_SUBS + sub) * n_diag
        q_heads = split_heads(sub)

        def more(carry):
            i, c_low = carry
            return (i < n_prev) & (c_low < SKIP_LOG2)

        def body(carry):
            i, _ = carry
            part, carries = sweep(q_heads, [(pl.multiple_of((n_prev - 1 - i) * TK, TK), None)],
                                  [c_ref[sub * n_heads + h] for h in range(n_heads)])
            for cg in range(SB_COLS):
                acc_ref[sub, :, cg * LANES:(cg + 1) * LANES] += part[cg]
            for h in range(n_heads):
                c_ref[sub * n_heads + h] = carries[h]
            return i + 1, lowest(carries)

        lax.while_loop(more, body, (jnp.int32(1), floors[sub]))
        o_ref[0, sub * TQ:(sub + 1) * TQ, :] = acc_ref[sub].astype(BF16)


def _stick_breaking(qkv):
    b, s, _ = qkv.shape
    width = SB_COLS * LANES
    n_col = MIX_WIDTH // width
    return pl.pallas_call(
        _sb_kernel,
        grid=(b, n_col, s // (SB_SUBS * TQ)),
        in_specs=[
            pl.BlockSpec((1, SB_SUBS * TQ, width), lambda i, p, j: (i, j, p)),
            pl.BlockSpec((1, s, width), lambda i, p, j: (i, 0, n_col + p)),
            pl.BlockSpec((1, s, width), lambda i, p, j: (i, 0, 2 * n_col + p)),
        ],
        out_specs=pl.BlockSpec((1, SB_SUBS * TQ, width), lambda i, p, j: (i, j, p)),
        out_shape=jax.ShapeDtypeStruct((b, s, MIX_WIDTH), BF16),
        scratch_shapes=[
            pltpu.VMEM((SB_SUBS, TQ, width), F32),
            pltpu.VMEM((SB_SUBS * SB_COLS * HEADS_PER_COL, TQ, 1), F32),
        ],
        compiler_params=pltpu.CompilerParams(
            dimension_semantics=("arbitrary", "arbitrary", "arbitrary"),
            vmem_limit_bytes=VMEM_LIMIT),
        name="stick_breaking",
    )(qkv, qkv, qkv)


def _out_ffn_kernel(x_ref, mix_ref, mo_ref, wo_ref, g_ref, wi_ref, wf_ref, out_ref):
    x1s, hs = [], []
    for sub in range(SUBTILES):
        rows = pl.ds(sub * TM, TM)
        x1 = (x_ref[0, rows, :] + _dot(mix_ref[0, rows, :], wo_ref[0:MIX_WIDTH, :])
              + _dot(mo_ref[0, rows, :], wo_ref[MIX_WIDTH:OUT_WIDTH, :]))
        x1s.append(x1)
        hs.append((_rms(x1) * g_ref[...]).astype(BF16))
    for sub in range(SUBTILES):
        rows = pl.ds(sub * TM, TM)
        h = hs[sub]
        ffn = None
        for lo, hi in FFN_SPLITS:
            gate = _dot(h, wi_ref[:, lo:hi])
            up = _dot(h, wi_ref[:, D_FF + lo:D_FF + hi])
            act = (gate * jax.nn.sigmoid(gate) * up).astype(BF16)
            part = _dot(act, wf_ref[lo:hi, :])
            ffn = part if ffn is None else ffn + part
        out_ref[0, rows, :] = ffn + x1s[sub]


def _out_ffn(layer, x, mixed, mem_out, w_out, g_ffn, w_ffn_in, w_ffn_out):
    b, s, _ = x.shape
    tok = lambda i, j: (i, j, 0)
    return pl.pallas_call(
        _out_ffn_kernel,
        grid=(b, s // TILE),
        in_specs=[
            pl.BlockSpec((1, TILE, D_MODEL), tok),
            pl.BlockSpec((1, TILE, MIX_WIDTH), tok),
            pl.BlockSpec((1, TILE, MEM_WIDTH), tok),
            _layer_param(layer, OUT_WIDTH, D_MODEL),
            _layer_param(layer, 1, D_MODEL),
            _layer_param(layer, D_MODEL, 2 * D_FF),
            _layer_param(layer, D_FF, D_MODEL),
        ],
        out_specs=pl.BlockSpec((1, TILE, D_MODEL), tok),
        out_shape=jax.ShapeDtypeStruct((b, s, D_MODEL), F32),
        compiler_params=pltpu.CompilerParams(
            dimension_semantics=("arbitrary", "arbitrary"), vmem_limit_bytes=VMEM_LIMIT),
        name="out_ffn",
    )(x, mixed, mem_out, w_out, g_ffn.reshape(DEPTH, 1, D_MODEL), w_ffn_in, w_ffn_out)


def kernel(x, mem, w_in, w_out, w_mem_kv, w_conv, w_ffn_in, w_ffn_out, g_mix, g_mem, g_ffn, g_q, g_k):
    assert x.shape[1] % TILE == 0 and x.shape[1] % (SB_SUBS * TQ) == 0 and TQ % TK == 0
    assert MIX_WIDTH % (SB_COLS * LANES) == 0
    w_in, w_out, w_mem_kv, w_ffn_in, w_ffn_out = (
        w.astype(BF16) for w in (w_in, w_out, w_mem_kv, w_ffn_in, w_ffn_out))
    knt, v = _mem_kv(mem, g_mem, w_mem_kv, g_k)
    for layer in range(DEPTH):
        mixed, mem_out = _in_proj(layer, x, g_mix, w_in, w_conv, g_q, knt, v)
        if layer % N_MIXERS == 1:
            mixed = _stick_breaking(mixed)
        x = _out_ffn(layer, x, mixed, mem_out, w_out, g_ffn, w_ffn_in, w_ffn_out)
    return x
```

```python
import math

import jax
import jax.numpy as jnp
from jax import lax
from jax.experimental import pallas as pl
from jax.experimental.pallas import tpu as pltpu

D_MODEL = 1024
DEPTH = 4
N_MIXERS = 2
CONV_WIDTH = 3
MIX_WIDTH = D_MODEL
SB_HEADS = 16
SB_HEAD_DIM = MIX_WIDTH // SB_HEADS
N_MEM = 256
MEM_HEADS = 4
MEM_HEAD_DIM = 128
MEM_WIDTH = MEM_HEADS * MEM_HEAD_DIM
IN_WIDTH = 3 * MIX_WIDTH + MEM_WIDTH
OUT_WIDTH = MIX_WIDTH + MEM_WIDTH
D_FF = -(-(8 * D_MODEL) // (3 * 256)) * 256
EPS = 1e-6

LANES = 128
SUBLANES = 8
VMEM_LIMIT = 56 * 1024 * 1024

TM = 512
SUBTILES = 2
TILE = SUBTILES * TM
TQ = 256
TK = 256
MXU_WIDTH = 256
FFN_SPLITS = ((0, 6 * MXU_WIDTH), (6 * MXU_WIDTH, D_FF))
HEADS_PER_COL = LANES // SB_HEAD_DIM
SB_COLS = 2
SB_SUBS = 8
LOG2E = math.log2(math.e)
SKIP_LOG2 = 160.0

BF16 = jnp.bfloat16
F32 = jnp.float32


def _dot(a, b):
    return jnp.dot(a, b, preferred_element_type=F32)


def _dot_nt(a, b):
    return lax.dot_general(a, b, (((1,), (1,)), ((), ())), preferred_element_type=F32)


def _rms(x):
    return x * lax.rsqrt(jnp.mean(x * x, axis=-1, keepdims=True) + EPS)


def _layer_param(layer, *block_shape):
    zeros = (0,) * len(block_shape)
    return pl.BlockSpec((None,) + block_shape, lambda i, j: (layer,) + zeros,
                        pipeline_mode=pl.Buffered(1))


def _mem_kv_kernel(mem_ref, g_ref, w_ref, gk_ref, knt_ref, v_ref):
    gk = gk_ref[0]
    for i in range(mem_ref.shape[0]):
        h = (_rms(mem_ref[i]) * g_ref[0]).astype(BF16)
        kv = _dot(h, w_ref[0])
        for hd in range(MEM_HEADS):
            lo = hd * MEM_HEAD_DIM
            kn = _rms(kv[:, lo:lo + MEM_HEAD_DIM]) * gk
            knt_ref[0, i, lo:lo + MEM_HEAD_DIM, :] = kn.T.astype(BF16)
        v_ref[0, i] = kv[:, MEM_WIDTH:].astype(BF16)


def _mem_kv(mem, g_mem, w_mem_kv, g_k):
    b = mem.shape[0]
    return pl.pallas_call(
        _mem_kv_kernel,
        grid=(DEPTH,),
        in_specs=[
            pl.BlockSpec((b, N_MEM, D_MODEL), lambda d: (0, 0, 0)),
            pl.BlockSpec((1, 1, D_MODEL), lambda d: (d, 0, 0)),
            pl.BlockSpec((1, D_MODEL, 2 * MEM_WIDTH), lambda d: (d, 0, 0)),
            pl.BlockSpec((1, 1, MEM_HEAD_DIM), lambda d: (d, 0, 0)),
        ],
        out_specs=[
            pl.BlockSpec((1, b, MEM_WIDTH, N_MEM), lambda d: (d, 0, 0, 0)),
            pl.BlockSpec((1, b, N_MEM, MEM_WIDTH), lambda d: (d, 0, 0, 0)),
        ],
        out_shape=[
            jax.ShapeDtypeStruct((DEPTH, b, MEM_WIDTH, N_MEM), BF16),
            jax.ShapeDtypeStruct((DEPTH, b, N_MEM, MEM_WIDTH), BF16),
        ],
        compiler_params=pltpu.CompilerParams(
            dimension_semantics=("arbitrary",), vmem_limit_bytes=VMEM_LIMIT),
        name="mem_kv",
    )(mem, g_mem.reshape(DEPTH, 1, D_MODEL), w_mem_kv, g_k.reshape(DEPTH, 1, MEM_HEAD_DIM))


def _mem_attention(h, w_ref, gq_ref, knt_ref, v_ref, out_ref, rows):
    qm = _dot(h, w_ref[:, 3 * MIX_WIDTH:])
    gq = gq_ref[...]
    for hd in range(MEM_HEADS):
        lo = hd * MEM_HEAD_DIM
        qn = (_rms(qm[:, lo:lo + MEM_HEAD_DIM]) * gq).astype(BF16)
        s = _dot(qn, knt_ref[0, lo:lo + MEM_HEAD_DIM, :]) * (MEM_HEAD_DIM ** -0.5)
        e = jnp.exp(s - jnp.max(s, axis=-1, keepdims=True))
        inv = 1.0 / jnp.sum(e, axis=-1, keepdims=True)
        o = _dot(e.astype(BF16), v_ref[0, :, lo:lo + MEM_HEAD_DIM]) * inv
        out_ref[0, rows, lo:lo + MEM_HEAD_DIM] = o.astype(BF16)


def _in_conv_kernel(x_ref, g_ref, w_ref, wc_ref, gq_ref, knt_ref, v_ref, mix_ref, mo_ref, ybuf):
    @pl.when(pl.program_id(1) == 0)
    def _():
        ybuf[0:SUBLANES, :] = jnp.zeros((SUBLANES, MIX_WIDTH), F32)

    for sub in range(SUBTILES):
        rows = pl.ds(sub * TM, TM)
        h = (_rms(x_ref[0, rows, :]) * g_ref[...]).astype(BF16)
        _mem_attention(h, w_ref, gq_ref, knt_ref, v_ref, mo_ref, rows)
        y = _dot(h, w_ref[:, MIX_WIDTH:2 * MIX_WIDTH]) * _dot(h, w_ref[:, 2 * MIX_WIDTH:3 * MIX_WIDTH])
        ybuf[SUBLANES:SUBLANES + TM, :] = y
        conv = (wc_ref[2:3, :] * y
                + wc_ref[1:2, :] * ybuf[SUBLANES - 1:SUBLANES - 1 + TM, :]
                + wc_ref[0:1, :] * ybuf[SUBLANES - 2:SUBLANES - 2 + TM, :])
        ybuf[0:SUBLANES, :] = ybuf[TM:TM + SUBLANES, :]
        mix_ref[0, rows, :] = (_dot(h, w_ref[:, 0:MIX_WIDTH]) * conv).astype(BF16)


def _in_sb_kernel(x_ref, g_ref, w_ref, gq_ref, knt_ref, v_ref, qkv_ref, mo_ref):
    for sub in range(SUBTILES):
        rows = pl.ds(sub * TM, TM)
        h = (_rms(x_ref[0, rows, :]) * g_ref[...]).astype(BF16)
        _mem_attention(h, w_ref, gq_ref, knt_ref, v_ref, mo_ref, rows)
        qkv_ref[0, rows, 0:MIX_WIDTH] = (
            _dot(h, w_ref[:, 0:MIX_WIDTH]) * (LOG2E * SB_HEAD_DIM ** -0.5)).astype(BF16)
        qkv_ref[0, rows, MIX_WIDTH:3 * MIX_WIDTH] = (
            _dot(h, w_ref[:, MIX_WIDTH:3 * MIX_WIDTH]).astype(BF16))


def _in_proj(layer, x, g_mix, w_in, w_conv, g_q, knt, v):
    b, s, _ = x.shape
    conv_layer = layer % N_MIXERS == 0
    tok = lambda i, j: (i, j, 0)
    in_specs = [
        pl.BlockSpec((1, TILE, D_MODEL), tok),
        _layer_param(layer, 1, D_MODEL),
        _layer_param(layer, D_MODEL, IN_WIDTH),
    ]
    args = [x, g_mix.reshape(DEPTH, 1, D_MODEL), w_in]
    if conv_layer:
        in_specs.append(_layer_param(layer // N_MIXERS, CONV_WIDTH, MIX_WIDTH))
        args.append(w_conv)
    in_specs += [
        _layer_param(layer, 1, MEM_HEAD_DIM),
        pl.BlockSpec((None, 1, MEM_WIDTH, N_MEM), lambda i, j: (layer, i, 0, 0)),
        pl.BlockSpec((None, 1, N_MEM, MEM_WIDTH), lambda i, j: (layer, i, 0, 0)),
    ]
    args += [g_q.reshape(DEPTH, 1, MEM_HEAD_DIM), knt, v]
    mix_width = MIX_WIDTH if conv_layer else 3 * MIX_WIDTH
    return pl.pallas_call(
        _in_conv_kernel if conv_layer else _in_sb_kernel,
        grid=(b, s // TILE),
        in_specs=in_specs,
        out_specs=[
            pl.BlockSpec((1, TILE, mix_width), tok),
            pl.BlockSpec((1, TILE, MEM_WIDTH), tok),
        ],
        out_shape=[
            jax.ShapeDtypeStruct((b, s, mix_width), BF16),
            jax.ShapeDtypeStruct((b, s, MEM_WIDTH), BF16),
        ],
        scratch_shapes=[pltpu.VMEM((SUBLANES + TM, MIX_WIDTH), F32)] if conv_layer else [],
        compiler_params=pltpu.CompilerParams(
            dimension_semantics=("arbitrary", "arbitrary"), vmem_limit_bytes=VMEM_LIMIT),
        name="in_conv" if conv_layer else "in_sb",
    )(*args)


def _sb_kernel(q_ref, k_ref, v_ref, o_ref, acc_ref, c_ref):
    step = pl.program_id(2)
    lane = lax.broadcasted_iota(jnp.int32, (1, LANES), 1)
    head_masks = [(lane >= hh * SB_HEAD_DIM) & (lane < (hh + 1) * SB_HEAD_DIM)
                  for hh in range(HEADS_PER_COL)]
    n_heads = SB_COLS * HEADS_PER_COL
    tri = (lax.broadcasted_iota(jnp.int32, (TK, TK), 0)
           >= lax.broadcasted_iota(jnp.int32, (TK, TK), 1)).astype(BF16)
    row = lax.broadcasted_iota(jnp.int32, (TQ, TK), 0)
    col = lax.broadcasted_iota(jnp.int32, (TQ, TK), 1)
    n_diag = TQ // TK

    def split_heads(sub):
        q_all = q_ref[0, sub * TQ:(sub + 1) * TQ, :]
        q_heads = []
        for cg in range(SB_COLS):
            qc = q_all[:, cg * LANES:(cg + 1) * LANES]
            q_heads += [jnp.where(m, qc, jnp.zeros_like(qc)) for m in head_masks]
        return q_heads

    def sweep(q_heads, blocks, carries):
        kbs = [k_ref[0, pl.ds(start, TK), :] for start, _ in blocks]
        vbs = [v_ref[0, pl.ds(start, TK), :] for start, _ in blocks]
        weights = [[None] * n_heads for _ in blocks]
        carries = list(carries)
        for h in range(n_heads):
            cg = h // HEADS_PER_COL
            for bi, (_, mask) in enumerate(blocks):
                z = _dot_nt(q_heads[h], kbs[bi][:, cg * LANES:(cg + 1) * LANES])
                sp = jnp.maximum(z, jnp.log2(1.0 + jnp.exp2(jnp.minimum(z, 126.0))))
                if mask is not None:
                    sp = jnp.where(mask, sp, 0.0)
                suffix = _dot(sp.astype(BF16), tri)
                a = jnp.exp2(z - suffix - carries[h])
                if mask is not None:
                    a = jnp.where(mask, a, 0.0)
                weights[bi][h] = a.astype(BF16)
                carries[h] = carries[h] + suffix[:, 0:1]
        outs = []
        for cg in range(SB_COLS):
            heads = range(cg * HEADS_PER_COL, (cg + 1) * HEADS_PER_COL)
            w_cat = jnp.concatenate([weights[bi][h] for bi in range(len(blocks)) for h in heads], axis=1)
            v_cat = jnp.concatenate(
                [jnp.where(m, vbs[bi][:, cg * LANES:(cg + 1) * LANES], jnp.zeros((TK, LANES), BF16))
                 for bi in range(len(blocks)) for m in head_masks], axis=0)
            outs.append(_dot(w_cat, v_cat))
        return outs, carries

    def lowest(carries):
        low = carries[0]
        for c in carries[1:]:
            low = jnp.minimum(low, c)
        return jnp.min(low)

    def head_block(sub, qi, with_previous):
        blocks = []
        for d in range(n_diag):
            off = (n_diag - 1 - d) * TK
            blocks.append((pl.multiple_of(qi * TQ + off, TK), col + off < row))
        if with_previous:
            blocks.append((pl.multiple_of((qi * n_diag - 1) * TK, TK), None))
        outs, carries = sweep(split_heads(sub), blocks, [jnp.zeros((TQ, 1), F32)] * n_heads)
        acc_ref[sub] = jnp.concatenate(outs, axis=1)
        for h in range(n_heads):
            c_ref[sub * n_heads + h] = carries[h]
        return lowest(carries)

    def later_step():
        return tuple(head_block(sub, step * SB_SUBS + sub, True) for sub in range(SB_SUBS))

    def first_step():
        return tuple(head_block(sub, sub, sub > 0) for sub in range(SB_SUBS))

    floors = lax.cond(step > 0, later_step, first_step)

    for sub in range(SB_SUBS):
        n_prev = (step * SB_SUBS + sub) * n_diag
        q_heads = split_heads(sub)

        def more(carry):
            i, c_low = carry
            return (i < n_prev) & (c_low < SKIP_LOG2)

        def body(carry):
            i, _ = carry
            part, carries = sweep(q_heads, [(pl.multiple_of((n_prev - 1 - i) * TK, TK), None)],
                                  [c_ref[sub * n_heads + h] for h in range(n_heads)])
            for cg in range(SB_COLS):
                acc_ref[sub, :, cg * LANES:(cg + 1) * LANES] += part[cg]
            for h in range(n_heads):
                c_ref[sub * n_heads + h] = carries[h]
            return i + 1, lowest(carries)

        lax.while_loop(more, body, (jnp.int32(1), floors[sub]))
        o_ref[0, sub * TQ:(sub + 1) * TQ, :] = acc_ref[sub].astype(BF16)


def _stick_breaking(qkv):
    b, s, _ = qkv.shape
    width = SB_COLS * LANES
    n_col = MIX_WIDTH // width
    return pl.pallas_call(
        _sb_kernel,
        grid=(b, n_col, s // (SB_SUBS * TQ)),
        in_specs=[
            pl.BlockSpec((1, SB_SUBS * TQ, width), lambda i, p, j: (i, j, p)),
            pl.BlockSpec((1, s, width), lambda i, p, j: (i, 0, n_col + p)),
            pl.BlockSpec((1, s, width), lambda i, p, j: (i, 0, 2 * n_col + p)),
        ],
        out_specs=pl.BlockSpec((1, SB_SUBS * TQ, width), lambda i, p, j: (i, j, p)),
        out_shape=jax.ShapeDtypeStruct((b, s, MIX_WIDTH), BF16),
        scratch_shapes=[
            pltpu.VMEM((SB_SUBS, TQ, width), F32),
            pltpu.VMEM((SB_SUBS * SB_COLS * HEADS_PER_COL, TQ, 1), F32),
        ],
        compiler_params=pltpu.CompilerParams(
            dimension_semantics=("arbitrary", "arbitrary", "arbitrary"),
            vmem_limit_bytes=VMEM_LIMIT),
        name="stick_breaking",
    )(qkv, qkv, qkv)


def _out_ffn_kernel(x_ref, mix_ref, mo_ref, wo_ref, g_ref, wi_ref, wf_ref, out_ref):
    x1s, hs = [], []
    for sub in range(SUBTILES):
        rows = pl.ds(sub * TM, TM)
        x1 = (x_ref[0, rows, :] + _dot(mix_ref[0, rows, :], wo_ref[0:MIX_WIDTH, :])
              + _dot(mo_ref[0, rows, :], wo_ref[MIX_WIDTH:OUT_WIDTH, :]))
        x1s.append(x1)
        hs.append((_rms(x1) * g_ref[...]).astype(BF16))
    for sub in range(SUBTILES):
        rows = pl.ds(sub * TM, TM)
        h = hs[sub]
        ffn = None
        for lo, hi in FFN_SPLITS:
            gate = _dot(h, wi_ref[:, lo:hi])
            up = _dot(h, wi_ref[:, D_FF + lo:D_FF + hi])
            act = (gate * jax.nn.sigmoid(gate) * up).astype(BF16)
            part = _dot(act, wf_ref[lo:hi, :])
            ffn = part if ffn is None else ffn + part
        out_ref[0, rows, :] = ffn + x1s[sub]


def _out_ffn(layer, x, mixed, mem_out, w_out, g_ffn, w_ffn_in, w_ffn_out):
    b, s, _ = x.shape
    tok = lambda i, j: (i, j, 0)
    return pl.pallas_call(
        _out_ffn_kernel,
        grid=(b, s // TILE),
        in_specs=[
            pl.BlockSpec((1, TILE, D_MODEL), tok),
            pl.BlockSpec((1, TILE, MIX_WIDTH), tok),
            pl.BlockSpec((1, TILE, MEM_WIDTH), tok),
            _layer_param(layer, OUT_WIDTH, D_MODEL),
            _layer_param(layer, 1, D_MODEL),
            _layer_param(layer, D_MODEL, 2 * D_FF),
            _layer_param(layer, D_FF, D_MODEL),
        ],
        out_specs=pl.BlockSpec((1, TILE, D_MODEL), tok),
        out_shape=jax.ShapeDtypeStruct((b, s, D_MODEL), F32),
        compiler_params=pltpu.CompilerParams(
            dimension_semantics=("arbitrary", "arbitrary"), vmem_limit_bytes=VMEM_LIMIT),
        name="out_ffn",
    )(x, mixed, mem_out, w_out, g_ffn.reshape(DEPTH, 1, D_MODEL), w_ffn_in, w_ffn_out)


def kernel(x, mem, w_in, w_out, w_mem_kv, w_conv, w_ffn_in, w_ffn_out, g_mix, g_mem, g_ffn, g_q, g_k):
    assert x.shape[1] % TILE == 0 and x.shape[1] % (SB_SUBS * TQ) == 0 and TQ % TK == 0
    assert MIX_WIDTH % (SB_COLS * LANES) == 0
    w_in, w_out, w_mem_kv, w_ffn_in, w_ffn_out = (
        w.astype(BF16) for w in (w_in, w_out, w_mem_kv, w_ffn_in, w_ffn_out))
    knt, v = _mem_kv(mem, g_mem, w_mem_kv, g_k)
    for layer in range(DEPTH):
        mixed, mem_out = _in_proj(layer, x, g_mix, w_in, w_conv, g_q, knt, v)
        if layer % N_MIXERS == 1:
            mixed = _stick_breaking(mixed)
        x = _out_ffn(layer, x, mixed, mem_out, w_out, g_ffn, w_ffn_in, w_ffn_out)
    return x
```

```python
import math

import jax
import jax.numpy as jnp
from jax import lax
from jax.experimental import pallas as pl
from jax.experimental.pallas import tpu as pltpu

D_MODEL = 1024
DEPTH = 4
N_MIXERS = 2
CONV_WIDTH = 3
MIX_WIDTH = D_MODEL
SB_HEADS = 16
SB_HEAD_DIM = MIX_WIDTH // SB_HEADS
N_MEM = 256
MEM_HEADS = 4
MEM_HEAD_DIM = 128
MEM_WIDTH = MEM_HEADS * MEM_HEAD_DIM
IN_WIDTH = 3 * MIX_WIDTH + MEM_WIDTH
OUT_WIDTH = MIX_WIDTH + MEM_WIDTH
D_FF = -(-(8 * D_MODEL) // (3 * 256)) * 256
EPS = 1e-6

LANES = 128
SUBLANES = 8
VMEM_LIMIT = 56 * 1024 * 1024

TM = 512
SUBTILES = 2
TILE = SUBTILES * TM
TQ = 256
TK = 256
MXU_WIDTH = 256
FFN_SPLITS = ((0, 6 * MXU_WIDTH), (6 * MXU_WIDTH, D_FF))
HEADS_PER_COL = LANES // SB_HEAD_DIM
SB_COLS = 2
SB_SUBS = 8
LOG2E = math.log2(math.e)
SKIP_LOG2 = 160.0
EXP2_CLAMP = 126.0

BF16 = jnp.bfloat16
F32 = jnp.float32


def _dot(a, b):
    return jnp.dot(a, b, preferred_element_type=F32)


def _dot_nt(a, b):
    return lax.dot_general(a, b, (((1,), (1,)), ((), ())), preferred_element_type=F32)


def _rms(x):
    return x * lax.rsqrt(jnp.mean(x * x, axis=-1, keepdims=True) + EPS)


def _layer_param(layer, *block_shape):
    zeros = (0,) * len(block_shape)
    return pl.BlockSpec((None,) + block_shape, lambda i, j: (layer,) + zeros,
                        pipeline_mode=pl.Buffered(1))


def _mem_kv_kernel(mem_ref, g_ref, w_ref, gk_ref, knt_ref, v_ref):
    gk = gk_ref[0]
    for i in range(mem_ref.shape[0]):
        h = (_rms(mem_ref[i]) * g_ref[0]).astype(BF16)
        kv = _dot(h, w_ref[0])
        for hd in range(MEM_HEADS):
            lo = hd * MEM_HEAD_DIM
            kn = _rms(kv[:, lo:lo + MEM_HEAD_DIM]) * gk
            knt_ref[0, i, lo:lo + MEM_HEAD_DIM, :] = kn.T.astype(BF16)
        v_ref[0, i] = kv[:, MEM_WIDTH:].astype(BF16)


def _mem_kv(mem, g_mem, w_mem_kv, g_k):
    b = mem.shape[0]
    return pl.pallas_call(
        _mem_kv_kernel,
        grid=(DEPTH,),
        in_specs=[
            pl.BlockSpec((b, N_MEM, D_MODEL), lambda d: (0, 0, 0)),
            pl.BlockSpec((1, 1, D_MODEL), lambda d: (d, 0, 0)),
            pl.BlockSpec((1, D_MODEL, 2 * MEM_WIDTH), lambda d: (d, 0, 0)),
            pl.BlockSpec((1, 1, MEM_HEAD_DIM), lambda d: (d, 0, 0)),
        ],
        out_specs=[
            pl.BlockSpec((1, b, MEM_WIDTH, N_MEM), lambda d: (d, 0, 0, 0)),
            pl.BlockSpec((1, b, N_MEM, MEM_WIDTH), lambda d: (d, 0, 0, 0)),
        ],
        out_shape=[
            jax.ShapeDtypeStruct((DEPTH, b, MEM_WIDTH, N_MEM), BF16),
            jax.ShapeDtypeStruct((DEPTH, b, N_MEM, MEM_WIDTH), BF16),
        ],
        compiler_params=pltpu.CompilerParams(
            dimension_semantics=("arbitrary",), vmem_limit_bytes=VMEM_LIMIT),
        name="mem_kv",
    )(mem, g_mem.reshape(DEPTH, 1, D_MODEL), w_mem_kv, g_k.reshape(DEPTH, 1, MEM_HEAD_DIM))


def _mem_attention(h, w_ref, gq_ref, knt_ref, v_ref, out_ref, rows):
    qm = _dot(h, w_ref[:, 3 * MIX_WIDTH:])
    gq = gq_ref[...]
    for hd in range(MEM_HEADS):
        lo = hd * MEM_HEAD_DIM
        qn = (_rms(qm[:, lo:lo + MEM_HEAD_DIM]) * gq).astype(BF16)
        s = _dot(qn, knt_ref[0, lo:lo + MEM_HEAD_DIM, :]) * (MEM_HEAD_DIM ** -0.5)
        e = jnp.exp(s - jnp.max(s, axis=-1, keepdims=True))
        inv = 1.0 / jnp.sum(e, axis=-1, keepdims=True)
        o = _dot(e.astype(BF16), v_ref[0, :, lo:lo + MEM_HEAD_DIM]) * inv
        out_ref[0, rows, lo:lo + MEM_HEAD_DIM] = o.astype(BF16)


def _in_conv_kernel(x_ref, g_ref, w_ref, wc_ref, gq_ref, knt_ref, v_ref, mix_ref, mo_ref, ybuf):
    @pl.when(pl.program_id(1) == 0)
    def _():
        ybuf[0:SUBLANES, :] = jnp.zeros((SUBLANES, MIX_WIDTH), F32)

    for sub in range(SUBTILES):
        rows = pl.ds(sub * TM, TM)
        h = (_rms(x_ref[0, rows, :]) * g_ref[...]).astype(BF16)
        _mem_attention(h, w_ref, gq_ref, knt_ref, v_ref, mo_ref, rows)
        y = _dot(h, w_ref[:, MIX_WIDTH:2 * MIX_WIDTH]) * _dot(h, w_ref[:, 2 * MIX_WIDTH:3 * MIX_WIDTH])
        ybuf[SUBLANES:SUBLANES + TM, :] = y
        conv = (wc_ref[2:3, :] * y
                + wc_ref[1:2, :] * ybuf[SUBLANES - 1:SUBLANES - 1 + TM, :]
                + wc_ref[0:1, :] * ybuf[SUBLANES - 2:SUBLANES - 2 + TM, :])
        ybuf[0:SUBLANES, :] = ybuf[TM:TM + SUBLANES, :]
        mix_ref[0, rows, :] = (_dot(h, w_ref[:, 0:MIX_WIDTH]) * conv).astype(BF16)


def _in_sb_kernel(x_ref, g_ref, w_ref, gq_ref, knt_ref, v_ref, qkv_ref, mo_ref):
    for sub in range(SUBTILES):
        rows = pl.ds(sub * TM, TM)
        h = (_rms(x_ref[0, rows, :]) * g_ref[...]).astype(BF16)
        _mem_attention(h, w_ref, gq_ref, knt_ref, v_ref, mo_ref, rows)
        qkv_ref[0, rows, 0:MIX_WIDTH] = (
            _dot(h, w_ref[:, 0:MIX_WIDTH]) * (LOG2E * SB_HEAD_DIM ** -0.5)).astype(BF16)
        qkv_ref[0, rows, MIX_WIDTH:3 * MIX_WIDTH] = (
            _dot(h, w_ref[:, MIX_WIDTH:3 * MIX_WIDTH]).astype(BF16))


def _in_proj(layer, x, g_mix, w_in, w_conv, g_q, knt, v):
    b, s, _ = x.shape
    conv_layer = layer % N_MIXERS == 0
    tok = lambda i, j: (i, j, 0)
    in_specs = [
        pl.BlockSpec((1, TILE, D_MODEL), tok),
        _layer_param(layer, 1, D_MODEL),
        _layer_param(layer, D_MODEL, IN_WIDTH),
    ]
    args = [x, g_mix.reshape(DEPTH, 1, D_MODEL), w_in]
    if conv_layer:
        in_specs.append(_layer_param(layer // N_MIXERS, CONV_WIDTH, MIX_WIDTH))
        args.append(w_conv)
    in_specs += [
        _layer_param(layer, 1, MEM_HEAD_DIM),
        pl.BlockSpec((None, 1, MEM_WIDTH, N_MEM), lambda i, j: (layer, i, 0, 0)),
        pl.BlockSpec((None, 1, N_MEM, MEM_WIDTH), lambda i, j: (layer, i, 0, 0)),
    ]
    args += [g_q.reshape(DEPTH, 1, MEM_HEAD_DIM), knt, v]
    mix_width = MIX_WIDTH if conv_layer else 3 * MIX_WIDTH
    return pl.pallas_call(
        _in_conv_kernel if conv_layer else _in_sb_kernel,
        grid=(b, s // TILE),
        in_specs=in_specs,
        out_specs=[
            pl.BlockSpec((1, TILE, mix_width), tok),
            pl.BlockSpec((1, TILE, MEM_WIDTH), tok),
        ],
        out_shape=[
            jax.ShapeDtypeStruct((b, s, mix_width), BF16),
            jax.ShapeDtypeStruct((b, s, MEM_WIDTH), BF16),
        ],
        scratch_shapes=[pltpu.VMEM((SUBLANES + TM, MIX_WIDTH), F32)] if conv_layer else [],
        compiler_params=pltpu.CompilerParams(
            dimension_semantics=("arbitrary", "arbitrary"), vmem_limit_bytes=VMEM_LIMIT),
        name="in_conv" if conv_layer else "in_sb",
    )(*args)


def _sb_kernel(q_ref, k_ref, v_ref, o_ref, acc_ref, c_ref):
    step = pl.program_id(2)
    lane = lax.broadcasted_iota(jnp.int32, (1, LANES), 1)
    head_masks = [(lane >= hh * SB_HEAD_DIM) & (lane < (hh + 1) * SB_HEAD_DIM)
                  for hh in range(HEADS_PER_COL)]
    n_heads = SB_COLS * HEADS_PER_COL
    tri = (lax.broadcasted_iota(jnp.int32, (TK, TK), 0)
           >= lax.broadcasted_iota(jnp.int32, (TK, TK), 1)).astype(BF16)
    row = lax.broadcasted_iota(jnp.int32, (TQ, TK), 0)
    col = lax.broadcasted_iota(jnp.int32, (TQ, TK), 1)
    n_diag = TQ // TK

    def split_heads(sub):
        q_all = q_ref[0, sub * TQ:(sub + 1) * TQ, :]
        q_heads = []
        for cg in range(SB_COLS):
            qc = q_all[:, cg * LANES:(cg + 1) * LANES]
            q_heads += [jnp.where(m, qc, jnp.zeros_like(qc)) for m in head_masks]
        return q_heads

    def sweep(q_heads, blocks, carries):
        kbs = [k_ref[0, pl.ds(start, TK), :] for start, _ in blocks]
        vbs = [v_ref[0, pl.ds(start, TK), :] for start, _ in blocks]
        weights = [[None] * n_heads for _ in blocks]
        carries = list(carries)
        for h in range(n_heads):
            cg = h // HEADS_PER_COL
            for bi, (_, mask) in enumerate(blocks):
                z = _dot_nt(q_heads[h], kbs[bi][:, cg * LANES:(cg + 1) * LANES])
                sp = jnp.maximum(z, jnp.log2(1.0 + jnp.exp2(jnp.minimum(z, EXP2_CLAMP))))
                if mask is not None:
                    sp = jnp.where(mask, sp, 0.0)
                suffix = _dot(sp.astype(BF16), tri)
                a = jnp.exp2(z - suffix - carries[h])
                if mask is not None:
                    a = jnp.where(mask, a, 0.0)
                weights[bi][h] = a.astype(BF16)
                carries[h] = carries[h] + suffix[:, 0:1]
        outs = []
        for cg in range(SB_COLS):
            heads = range(cg * HEADS_PER_COL, (cg + 1) * HEADS_PER_COL)
            w_cat = jnp.concatenate([weights[bi][h] for bi in range(len(blocks)) for h in heads], axis=1)
            v_cat = jnp.concatenate(
                [jnp.where(m, vbs[bi][:, cg * LANES:(cg + 1) * LANES], jnp.zeros((TK, LANES), BF16))
                 for bi in range(len(blocks)) for m in head_masks], axis=0)
            outs.append(_dot(w_cat, v_cat))
        return outs, carries

    def lowest(carries):
        low = carries[0]
        for c in carries[1:]:
            low = jnp.minimum(low, c)
        return jnp.min(low)

    def head_block(sub, qi, with_previous):
        blocks = []
        for d in range(n_diag):
            off = (n_diag - 1 - d) * TK
            blocks.append((pl.multiple_of(qi * TQ + off, TK), col + off < row))
        if with_previous:
            blocks.append((pl.multiple_of((qi * n_diag - 1) * TK, TK), None))
        outs, carries = sweep(split_heads(sub), blocks, [jnp.zeros((TQ, 1), F32)] * n_heads)
        acc_ref[sub] = jnp.concatenate(outs, axis=1)
        for h in range(n_heads):
            c_ref[sub * n_heads + h] = carries[h]
        return lowest(carries)

    def later_step():
        return tuple(head_block(sub, step * SB_SUBS + sub, True) for sub in range(SB_SUBS))

    def first_step():
        return tuple(head_block(sub, sub, sub > 0) for sub in range(SB_SUBS))

    floors = lax.cond(step > 0, later_step, first_step)

    for sub in range(SB_SUBS):
        n_prev = (step * SB_SUBS + sub) * n_diag
        q_heads = split_heads(sub)

        def more(carry):
            i, c_low = carry
            return (i < n_prev) & (c_low < SKIP_LOG2)

        def body(carry):
            i, _ = carry
            part, carries = sweep(q_heads, [(pl.multiple_of((n_prev - 1 - i) * TK, TK), None)],
                                  [c_ref[sub * n_heads + h] for h in range(n_heads)])
            for cg in range(SB_COLS):
                acc_ref[sub, :, cg * LANES:(cg + 1) * LANES] += part[cg]
            for h in range(n_heads):
                c_ref[sub * n_heads + h] = carries[h]
            return i + 1, lowest(carries)

        lax.while_loop(more, body, (jnp.int32(1), floors[sub]))
        o_ref[0, sub * TQ:(sub + 1) * TQ, :] = acc_ref[sub].astype(BF16)


def _stick_breaking(qkv):
    b, s, _ = qkv.shape
    width = SB_COLS * LANES
    n_col = MIX_WIDTH // width
    return pl.pallas_call(
        _sb_kernel,
        grid=(b, n_col, s // (SB_SUBS * TQ)),
        in_specs=[
            pl.BlockSpec((1, SB_SUBS * TQ, width), lambda i, p, j: (i, j, p)),
            pl.BlockSpec((1, s, width), lambda i, p, j: (i, 0, n_col + p)),
            pl.BlockSpec((1, s, width), lambda i, p, j: (i, 0, 2 * n_col + p)),
        ],
        out_specs=pl.BlockSpec((1, SB_SUBS * TQ, width), lambda i, p, j: (i, j, p)),
        out_shape=jax.ShapeDtypeStruct((b, s, MIX_WIDTH), BF16),
        scratch_shapes=[
            pltpu.VMEM((SB_SUBS, TQ, width), F32),
            pltpu.VMEM((SB_SUBS * SB_COLS * HEADS_PER_COL, TQ, 1), F32),
        ],
        compiler_params=pltpu.CompilerParams(
            dimension_semantics=("arbitrary", "arbitrary", "arbitrary"),
            vmem_limit_bytes=VMEM_LIMIT),
        name="stick_breaking",
    )(qkv, qkv, qkv)


def _out_ffn_kernel(x_ref, mix_ref, mo_ref, wo_ref, g_ref, wi_ref, wf_ref, out_ref):
    x1s, hs = [], []
    for sub in range(SUBTILES):
        rows = pl.ds(sub * TM, TM)
        x1 = (x_ref[0, rows, :] + _dot(mix_ref[0, rows, :], wo_ref[0:MIX_WIDTH, :])
              + _dot(mo_ref[0, rows, :], wo_ref[MIX_WIDTH:OUT_WIDTH, :]))
        x1s.append(x1)
        hs.append((_rms(x1) * g_ref[...]).astype(BF16))
    for sub in range(SUBTILES):
        rows = pl.ds(sub * TM, TM)
        h = hs[sub]
        ffn = None
        for lo, hi in FFN_SPLITS:
            gate = _dot(h, wi_ref[:, lo:hi])
            up = _dot(h, wi_ref[:, D_FF + lo:D_FF + hi])
            act = (gate * jax.nn.sigmoid(gate) * up).astype(BF16)
            part = _dot(act, wf_ref[lo:hi, :])
            ffn = part if ffn is None else ffn + part
        out_ref[0, rows, :] = ffn + x1s[sub]


def _out_ffn(layer, x, mixed, mem_out, w_out, g_ffn, w_ffn_in, w_ffn_out):
    b, s, _ = x.shape
    tok = lambda i, j: (i, j, 0)
    return pl.pallas_call(
        _out_ffn_kernel,
        grid=(b, s // TILE),
        in_specs=[
            pl.BlockSpec((1, TILE, D_MODEL), tok),
            pl.BlockSpec((1, TILE, MIX_WIDTH), tok),
            pl.BlockSpec((1, TILE, MEM_WIDTH), tok),
            _layer_param(layer, OUT_WIDTH, D_MODEL),
            _layer_param(layer, 1, D_MODEL),
            _layer_param(layer, D_MODEL, 2 * D_FF),
            _layer_param(layer, D_FF, D_MODEL),
        ],
        out_specs=pl.BlockSpec((1, TILE, D_MODEL), tok),
        out_shape=jax.ShapeDtypeStruct((b, s, D_MODEL), F32),
        compiler_params=pltpu.CompilerParams(
            dimension_semantics=("arbitrary", "arbitrary"), vmem_limit_bytes=VMEM_LIMIT),
        name="out_ffn",
    )(x, mixed, mem_out, w_out, g_ffn.reshape(DEPTH, 1, D_MODEL), w_ffn_in, w_ffn_out)


def kernel(x, mem, w_in, w_out, w_mem_kv, w_conv, w_ffn_in, w_ffn_out, g_mix, g_mem, g_ffn, g_q, g_k):
    assert x.shape[1] % TILE == 0 and x.shape[1] % (SB_SUBS * TQ) == 0 and TQ % TK == 0
    assert MIX_WIDTH % (SB_COLS * LANES) == 0
    w_in, w_out, w_mem_kv, w_ffn_in, w_ffn_out = (
        w.astype(BF16) for w in (w_in, w_out, w_mem_kv, w_ffn_in, w_ffn_out))
    knt, v = _mem_kv(mem, g_mem, w_mem_kv, g_k)
    for layer in range(DEPTH):
        mixed, mem_out = _in_proj(layer, x, g_mix, w_in, w_conv, g_q, knt, v)
        if layer % N_MIXERS == 1:
            mixed = _stick_breaking(mixed)
        x = _out_ffn(layer, x, mixed, mem_out, w_out, g_ffn, w_ffn_in, w_ffn_out)
    return x
```
